```python
import math
import jax, jax.numpy as jnp
from jax import lax
import numpy as np

D_MODEL = 1024
BATCH = 2
SEQ = 8192
DEPTH = 1
DEC_BATCH = 128
DEC_SEQ = 4
PAST_LEN = 8192
PAGE_SIZE = 128

N_META = 16
GLA_HEADS = 4
GLA_DK = D_MODEL // 2 // GLA_HEADS
GLA_DV = D_MODEL // GLA_HEADS
GLA_RANK = 16
GLA_TAU = 16.0
GLA_CHUNK = 64
SB_HEADS = 16
SB_DH = D_MODEL // SB_HEADS
SB_BLOCK = 128
SB_SCALE = SB_DH ** -0.5
SB_BIAS_INIT = -6.0
D_FF = 4 * D_MODEL
DN_ALPHA = (2.0 * DEPTH) ** 0.25
DN_BETA = (8.0 * DEPTH) ** -0.25
LN_EPS = 1e-5
POOL_FACTOR = 1.25

GQK = GLA_HEADS * GLA_DK
GV = GLA_HEADS * GLA_DV
SBW = SB_HEADS * SB_DH
SPLITS = (GQK, GQK, GV, GV, GLA_RANK, SBW, SBW, SBW, D_MODEL, D_MODEL)
D_IN = sum(SPLITS)

kernel_name = "gla_stickbreak_gated_parallel_decoder_step"


def _split_projection(z):
    offsets = [int(o) for o in np.cumsum(SPLITS)[:-1]]
    return jnp.split(z, offsets, axis=-1)


def _layer_norm(x, g, b):
    xf = x.astype(jnp.float32)
    mu = jnp.mean(xf, axis=-1, keepdims=True)
    var = jnp.mean(jnp.square(xf - mu), axis=-1, keepdims=True)
    return ((xf - mu) * lax.rsqrt(var + LN_EPS) * g + b).astype(x.dtype)


def _project(x, w_in, w_gate_a2, b_gate_a):
    bsz, length = x.shape[0], x.shape[1]
    q_a, k_a, v_a, r_a, a_lr, q_b, k_b, v_b, g_a, g_b = _split_projection(x @ w_in)
    q_a = q_a.reshape(bsz, length, GLA_HEADS, GLA_DK) * (GLA_DK ** -0.5)
    k_a = k_a.reshape(bsz, length, GLA_HEADS, GLA_DK)
    v_a = v_a.reshape(bsz, length, GLA_HEADS, GLA_DV)
    log_a = jax.nn.log_sigmoid((a_lr @ w_gate_a2 + b_gate_a).astype(jnp.float32)) / GLA_TAU
    log_a = log_a.reshape(bsz, length, GLA_HEADS, GLA_DK)
    q_b = q_b.reshape(bsz, length, SB_HEADS, SB_DH)
    k_b = k_b.reshape(bsz, length, SB_HEADS, SB_DH)
    v_b = v_b.reshape(bsz, length, SB_HEADS, SB_DH)
    return q_a, k_a, v_a, r_a, log_a, q_b, k_b, v_b, g_a, g_b


def _gla_chunk(s0, q, k, v, log_a):
    q, k, v = (t.astype(jnp.float32) for t in (q, k, v))
    length = q.shape[1]
    b = jnp.cumsum(log_a, axis=1)
    causal = jnp.tril(jnp.ones((length, length), dtype=bool))
    diff = b[:, :, None] - b[:, None, :]
    decay = jnp.exp(jnp.where(causal[None, :, :, None, None], diff, -jnp.inf))
    scores = jnp.einsum("bthk,bshk,btshk->bhts", q, k, decay)
    o = jnp.einsum("bhts,bshv->bthv", scores, v) + jnp.einsum("bthk,bhkv->bthv", q * jnp.exp(b), s0)
    b_last = b[:, -1]
    s_new = jnp.exp(b_last)[..., None] * s0 + jnp.einsum("bshk,bshv->bhkv", k * jnp.exp(b_last[:, None] - b), v)
    return s_new, o


def _gla_prompt(q, k, v, log_a):
    bsz = q.shape[0]
    s0 = jnp.zeros((bsz, GLA_HEADS, GLA_DK, GLA_DV), jnp.float32)
    s, o_meta = _gla_chunk(s0, q[:, :N_META], k[:, :N_META], v[:, :N_META], log_a[:, :N_META])
    n_chunks = (q.shape[1] - N_META) // GLA_CHUNK

    def to_chunks(t):
        t = t[:, N_META:]
        return jnp.moveaxis(t.reshape(bsz, n_chunks, GLA_CHUNK, *t.shape[2:]), 1, 0)

    def step(carry, xs):
        return _gla_chunk(carry, *xs)

    s, o_rest = lax.scan(step, s, tuple(to_chunks(t) for t in (q, k, v, log_a)))
    o_rest = jnp.moveaxis(o_rest, 0, 1).reshape(bsz, n_chunks * GLA_CHUNK, GLA_HEADS, GLA_DV)
    return s, jnp.concatenate([o_meta, o_rest], axis=1)


def _gla_output(o, r, gain):
    bsz, length = o.shape[0], o.shape[1]
    o = o * lax.rsqrt(jnp.mean(jnp.square(o), axis=-1, keepdims=True) + LN_EPS)
    o = o * gain.astype(jnp.float32).reshape(GLA_HEADS, GLA_DV) * jax.nn.silu(
        r.astype(jnp.float32).reshape(bsz, length, GLA_HEADS, GLA_DV))
    return o.reshape(bsz, length, GV).astype(r.dtype)


def _stick_breaking(z, valid):
    log_keep = jnp.where(valid, jax.nn.log_sigmoid(-z), 0.0)
    between = lax.cumsum(log_keep, axis=z.ndim - 1, reverse=True) - log_keep
    return jnp.where(valid, jnp.exp(jax.nn.log_sigmoid(z) + between), 0.0)


def _sb_prompt(q, k, v, sb_bias):
    bsz, length = q.shape[0], q.shape[1]
    n_blk = -(-length // SB_BLOCK)
    pad = n_blk * SB_BLOCK - length
    q_blocks = jnp.pad(q, ((0, 0), (0, pad), (0, 0), (0, 0))).reshape(
        bsz, n_blk, SB_BLOCK, SB_HEADS, SB_DH).swapaxes(0, 1)
    kf = k.astype(jnp.float32)
    vf = v.astype(jnp.float32)
    bias = sb_bias.astype(jnp.float32)[None, :, None, None]
    s_idx = jnp.arange(length)

    def block(args):
        qb, t0 = args
        t_idx = t0 + jnp.arange(SB_BLOCK)
        z = jnp.einsum("bthd,bshd->bhts", qb.astype(jnp.float32), kf) * SB_SCALE + bias
        valid = s_idx[None, :] < t_idx[:, None]
        return jnp.einsum("bhts,bshd->bthd", _stick_breaking(z, valid), vf)

    o = lax.map(block, (q_blocks, jnp.arange(n_blk, dtype=jnp.int32) * SB_BLOCK))
    o = o.swapaxes(0, 1).reshape(bsz, n_blk * SB_BLOCK, SB_HEADS, SB_DH)[:, :length]
    return o.reshape(bsz, length, SBW).astype(q.dtype)


def _sb_sample(q, k, v, cache_k, cache_v, layer, page_table, sb_bias):
    past = page_table.shape[1] * PAGE_SIZE
    new = q.shape[1]
    valid = jnp.arange(past + new)[None, :] < (past + jnp.arange(new))[:, None]
    bias = sb_bias.astype(jnp.float32)[:, None, None]
    pool_k = cache_k[layer]
    pool_v = cache_v[layer]

    def one(args):
        pages, qs, ks, vs = args
        keys = jnp.concatenate([jnp.take(pool_k, pages, axis=0).reshape(past, SB_HEADS, SB_DH).astype(jnp.float32),
                                ks.astype(jnp.float32)], axis=0)
        vals = jnp.concatenate([jnp.take(pool_v, pages, axis=0).reshape(past, SB_HEADS, SB_DH).astype(jnp.float32),
                                vs.astype(jnp.float32)], axis=0)
        z = jnp.einsum("thd,shd->hts", qs.astype(jnp.float32), keys) * SB_SCALE + bias
        return jnp.einsum("hts,shd->thd", _stick_breaking(z, valid), vals)

    o = lax.map(one, (page_table, q, k, v))
    return o.reshape(q.shape[0], new, SBW).astype(q.dtype)


def _merge_and_ffn(x, o_a, o_b, g_a, g_b, w_o, ln1_g, ln1_b, w_up, w_down, ln2_g, ln2_b):
    mix = (jax.nn.sigmoid(g_a) * o_a + jax.nn.sigmoid(g_b) * o_b) @ w_o
    h = _layer_norm(DN_ALPHA * x + mix, ln1_g, ln1_b)
    f = jnp.square(jax.nn.relu(h @ w_up)) @ w_down
    return _layer_norm(DN_ALPHA * h + f, ln2_g, ln2_b)


def setup_inputs(seed: int = 0) -> dict:
    key = jax.random.key(seed)
    ks = jax.random.split(key, 20)
    n_pages = PAST_LEN // PAGE_SIZE
    n_used = DEC_BATCH * n_pages
    n_phys = int(math.ceil(POOL_FACTOR * n_used))
    f32 = jnp.float32
    x_prompt = jax.random.normal(ks[0], (BATCH, SEQ, D_MODEL), f32)
    x_sample = jax.random.normal(ks[1], (DEC_BATCH, DEC_SEQ, D_MODEL), f32)
    cache_sb_k = jax.random.normal(ks[2], (DEPTH, n_phys, PAGE_SIZE, SB_HEADS, SB_DH), f32)
    cache_sb_v = jax.random.normal(ks[3], (DEPTH, n_phys, PAGE_SIZE, SB_HEADS, SB_DH), f32)
    state_gla = 0.5 * jax.random.normal(ks[4], (DEPTH, DEC_BATCH, GLA_HEADS, GLA_DK, GLA_DV), f32)
    page_table = jax.random.permutation(ks[5], n_phys)[:n_used].reshape(DEC_BATCH, n_pages).astype(jnp.int32)
    meta_tokens = jax.random.normal(ks[6], (N_META, D_MODEL), f32)
    w_in = jax.random.normal(ks[7], (DEPTH, D_MODEL, D_IN), f32) * D_MODEL ** -0.5
    w_gate_a2 = jax.random.normal(ks[8], (DEPTH, GLA_RANK, GQK), f32) * GLA_RANK ** -0.5
    b_gate_a = 0.1 * jax.random.normal(ks[9], (DEPTH, GQK), f32)
    gla_norm_g = 1.0 + 0.02 * jax.random.normal(ks[10], (DEPTH, GV), f32)
    sb_logit_bias = SB_BIAS_INIT + 0.1 * jax.random.normal(ks[18], (DEPTH, SB_HEADS), f32)
    w_o = jax.random.normal(ks[11], (DEPTH, D_MODEL, D_MODEL), f32) * (D_MODEL ** -0.5 * DN_BETA)
    ln1_g = 1.0 + 0.02 * jax.random.normal(ks[12], (DEPTH, D_MODEL), f32)
    ln1_b = 0.02 * jax.random.normal(ks[13], (DEPTH, D_MODEL), f32)
    w_up = jax.random.normal(ks[14], (DEPTH, D_MODEL, D_FF), f32) * D_MODEL ** -0.5
    w_down = jax.random.normal(ks[15], (DEPTH, D_FF, D_MODEL), f32) * (D_FF ** -0.5 * DN_BETA)
    ln2_g = 1.0 + 0.02 * jax.random.normal(ks[16], (DEPTH, D_MODEL), f32)
    ln2_b = 0.02 * jax.random.normal(ks[17], (DEPTH, D_MODEL), f32)
    return {"x_prompt": x_prompt, "x_sample": x_sample, "cache_sb_k": cache_sb_k, "cache_sb_v": cache_sb_v,
            "state_gla": state_gla, "page_table": page_table, "meta_tokens": meta_tokens, "w_in": w_in,
            "w_gate_a2": w_gate_a2, "b_gate_a": b_gate_a, "gla_norm_g": gla_norm_g,
            "sb_logit_bias": sb_logit_bias, "w_o": w_o,
            "ln1_g": ln1_g, "ln1_b": ln1_b, "w_up": w_up, "w_down": w_down, "ln2_g": ln2_g, "ln2_b": ln2_b}


def reference(x_prompt, x_sample, cache_sb_k, cache_sb_v, state_gla, page_table, meta_tokens, w_in,
              w_gate_a2, b_gate_a, gla_norm_g, sb_logit_bias, w_o, ln1_g, ln1_b, w_up, w_down, ln2_g, ln2_b):
    bsz = x_prompt.shape[0]
    meta = jnp.broadcast_to(meta_tokens.astype(x_prompt.dtype)[None], (bsz, N_META, D_MODEL))
    xp = jnp.concatenate([meta, x_prompt], axis=1)
    xs = x_sample
    kp_list, vp_list, sp_list, ks_list, vs_list, ss_list = [], [], [], [], [], []
    for l in range(DEPTH):
        q_a, k_a, v_a, r_a, log_a, q_b, k_b, v_b, g_a, g_b = _project(xp, w_in[l], w_gate_a2[l], b_gate_a[l])
        s_p, o_a = _gla_prompt(q_a, k_a, v_a, log_a)
        o_a = _gla_output(o_a, r_a, gla_norm_g[l])
        o_b = _sb_prompt(q_b, k_b, v_b, sb_logit_bias[l])
        kp_list.append(k_b)
        vp_list.append(v_b)
        sp_list.append(s_p.astype(state_gla.dtype))
        xp = _merge_and_ffn(xp, o_a, o_b, g_a, g_b, w_o[l], ln1_g[l], ln1_b[l], w_up[l], w_down[l], ln2_g[l], ln2_b[l])
        q_a, k_a, v_a, r_a, log_a, q_b, k_b, v_b, g_a, g_b = _project(xs, w_in[l], w_gate_a2[l], b_gate_a[l])
        s_s, o_a = _gla_chunk(state_gla[l].astype(jnp.float32), q_a, k_a, v_a, log_a)
        o_a = _gla_output(o_a, r_a, gla_norm_g[l])
        o_b = _sb_sample(q_b, k_b, v_b, cache_sb_k, cache_sb_v, l, page_table, sb_logit_bias[l])
        ks_list.append(k_b)
        vs_list.append(v_b)
        ss_list.append(s_s.astype(state_gla.dtype))
        xs = _merge_and_ffn(xs, o_a, o_b, g_a, g_b, w_o[l], ln1_g[l], ln1_b[l], w_up[l], w_down[l], ln2_g[l], ln2_b[l])
    y_prompt = xp[:, N_META:]
    new_k_prompt = jnp.stack(kp_list)
    new_v_prompt = jnp.stack(vp_list)
    new_gla_prompt = jnp.stack(sp_list)
    new_k_sample = jnp.stack(ks_list)
    new_v_sample = jnp.stack(vs_list)
    new_gla_sample = jnp.stack(ss_list)
    return (y_prompt, xs, new_k_prompt, new_v_prompt, new_gla_prompt, new_k_sample, new_v_sample, new_gla_sample)
```

```python
import functools

import numpy as np
import jax
import jax.numpy as jnp
from jax import lax
from jax.experimental import pallas as pl
from jax.experimental.pallas import tpu as pltpu

F32 = jnp.float32
BF16 = jnp.bfloat16

N_META = 16
GLA_HEADS = 4
GLA_RANK = 16
GLA_TAU = 16.0
SB_HEADS = 16
PAGE_SIZE = 128
LN_EPS = 1e-5
DEPTH = 1
DN_ALPHA = (2.0 * DEPTH) ** 0.25

LANES = 128
MXU_DIM = 256
VMEM_LIMIT_BYTES = 56 * 1024 * 1024

GLA_CHUNK = 128
SB_BLOCK = 256
ROW_ALIGN = 256
SAMPLE_PAD = 16


def _dot(a, b):
    return jnp.dot(a, b, preferred_element_type=F32)


def _dot_nt(a, b):
    return lax.dot_general(a, b, (((1,), (1,)), ((), ())), preferred_element_type=F32)


def _split_bf16(x):
    hi = x.astype(BF16)
    lo = (x - hi.astype(F32)).astype(BF16)
    return hi, lo


def _softplus(z):
    return jnp.maximum(z, 0.0) + jnp.log(1.0 + jnp.exp(-jnp.abs(z)))


def _sigmoid(z):
    return 1.0 / (1.0 + jnp.exp(-z))


def _layer_norm(x, g, b):
    mu = jnp.mean(x, axis=-1, keepdims=True)
    xc = x - mu
    var = jnp.mean(xc * xc, axis=-1, keepdims=True)
    return xc * lax.rsqrt(var + LN_EPS) * g + b


def _params(*sem):
    return pltpu.CompilerParams(dimension_semantics=sem, vmem_limit_bytes=VMEM_LIMIT_BYTES)


def _full(shape):
    return pl.BlockSpec(shape, lambda *_: (0,) * len(shape))


def _proj_gla_kernel(x_ref, wq_ref, wk_ref, wv_ref, wr_ref, wa_ref, wg_ref, bg_ref,
                     q_ref, k_ref, v_ref, r_ref, la_ref, *, q_scale):
    x = x_ref[...].astype(BF16)
    q_ref[...] = _dot(x, wq_ref[...]) * q_scale
    k_ref[...] = _dot(x, wk_ref[...])
    v_ref[...] = _dot(x, wv_ref[...]).astype(BF16)
    r = _dot(x, wr_ref[...])
    r_ref[...] = r * _sigmoid(r)
    a = _dot(x, wa_ref[...]).astype(BF16)
    zg = _dot(a, wg_ref[...]) + bg_ref[...]
    la_ref[...] = -_softplus(-zg) * (1.0 / GLA_TAU)


def _proj_gla(x, wq, wk, wv, wr, wa, wg, bg, tm):
    n, d = x.shape
    gqk, gv = wq.shape[1], wv.shape[1]
    dk = gqk // GLA_HEADS
    row = lambda c: pl.BlockSpec((tm, c), lambda i: (i, 0))
    return pl.pallas_call(
        functools.partial(_proj_gla_kernel, q_scale=dk ** -0.5),
        grid=(n // tm,),
        in_specs=[row(d), _full(wq.shape), _full(wk.shape), _full(wv.shape), _full(wr.shape),
                  _full(wa.shape), _full(wg.shape), _full(bg.shape)],
        out_specs=[row(gqk), row(gqk), row(gv), row(gv), row(gqk)],
        out_shape=[jax.ShapeDtypeStruct((n, gqk), F32), jax.ShapeDtypeStruct((n, gqk), F32),
                   jax.ShapeDtypeStruct((n, gv), BF16), jax.ShapeDtypeStruct((n, gv), F32),
                   jax.ShapeDtypeStruct((n, gqk), F32)],
        compiler_params=_params("parallel"),
        name="proj_gla",
    )(x, wq, wk, wv, wr, wa, wg, bg)


def _proj_sb_kernel(x_ref, wq_ref, wk_ref, wv_ref, q_ref, k_ref, v_ref, k16_ref, v16_ref, *, q_scale):
    x = x_ref[...].astype(BF16)
    q_ref[...] = (_dot(x, wq_ref[...]) * q_scale).astype(BF16)
    k = _dot(x, wk_ref[...])
    k_ref[...] = k
    k16_ref[...] = k.astype(BF16)
    v = _dot(x, wv_ref[...])
    v_ref[...] = v
    v16_ref[...] = v.astype(BF16)


def _proj_sb(x, wq, wk, wv, tm):
    n, d = x.shape
    w = wq.shape[1]
    dh = w // SB_HEADS
    row = pl.BlockSpec((tm, w), lambda i: (i, 0))
    return pl.pallas_call(
        functools.partial(_proj_sb_kernel, q_scale=dh ** -0.5),
        grid=(n // tm,),
        in_specs=[pl.BlockSpec((tm, d), lambda i: (i, 0)), _full(wq.shape), _full(wk.shape), _full(wv.shape)],
        out_specs=[row, row, row, row, row],
        out_shape=[jax.ShapeDtypeStruct((n, w), BF16), jax.ShapeDtypeStruct((n, w), F32),
                   jax.ShapeDtypeStruct((n, w), F32), jax.ShapeDtypeStruct((n, w), BF16),
                   jax.ShapeDtypeStruct((n, w), BF16)],
        compiler_params=_params("parallel"),
        name="proj_sb",
    )(x, wq, wk, wv)


def _gla_sum_matrices(c):
    t = np.arange(c)
    j = t[None, :]
    mats = [j <= t[:, None], j > t[:, None]]
    l = 1
    while l < c:
        start = (t // l) * l
        mats.append((j > start[:, None]) & (j <= t[:, None]))
        nxt = np.minimum((t // l + 1) * l, c - 1)
        mats.append((j > t[:, None]) & (j <= nxt[:, None]))
        l *= 2
    return np.concatenate(mats, axis=0).astype(np.float32)


def _gla_kernel(m_ref, q_ref, k_ref, v_ref, r_ref, la_ref, g_ref, s0_ref, o_ref, s_out_ref,
                e_ref, s_ref, *, chunk, n_levels, dk, dv):
    c = pl.program_id(1)

    @pl.when(c == 0)
    def _():
        s_ref[...] = s0_ref[0]

    la_hi, la_lo = _split_bf16(la_ref[0])
    m = m_ref[...]
    e_ref[...] = _dot(m, la_hi) + _dot(m, la_lo)

    rows = lax.broadcasted_iota(jnp.int32, (chunk, chunk), 0)
    cols = lax.broadcasted_iota(jnp.int32, (chunk, chunk), 1)
    level = 31 - lax.clz(rows ^ cols)
    level = jnp.where(cols < rows, level, -1)
    diag = rows == cols

    for h in range(GLA_HEADS):
        ks = slice(h * dk, (h + 1) * dk)
        vs = slice(h * dv, (h + 1) * dv)
        q = q_ref[0][:, ks]
        k = k_ref[0][:, ks]
        v = v_ref[0][:, vs]
        b = e_ref[0:chunk, ks]
        tail = e_ref[chunk:2 * chunk, ks]
        scores = jnp.where(diag, _dot_nt(q.astype(BF16), k.astype(BF16)), 0.0)
        for li in range(n_levels):
            eq = e_ref[(2 + 2 * li) * chunk:(3 + 2 * li) * chunk, ks]
            ek = e_ref[(3 + 2 * li) * chunk:(4 + 2 * li) * chunk, ks]
            r = _dot_nt((q * jnp.exp(eq)).astype(BF16), (k * jnp.exp(ek)).astype(BF16))
            scores = jnp.where(level == li, r, scores)
        s_prev = s_ref[h]
        o = _dot(scores.astype(BF16), v) + _dot((q * jnp.exp(b)).astype(BF16), s_prev.astype(BF16))
        k_out_t = (k * jnp.exp(tail)).T.astype(BF16)
        decay_col = jnp.exp(b.T[:, chunk - 1:chunk])
        s_ref[h] = decay_col * s_prev + _dot(k_out_t, v)
        o = o * lax.rsqrt(jnp.mean(o * o, axis=-1, keepdims=True) + LN_EPS)
        o_ref[0, :, vs] = o * g_ref[:, vs] * r_ref[0][:, vs]

    @pl.when(c == pl.num_programs(1) - 1)
    def _():
        s_out_ref[0] = s_ref[...]


def _gla(q, k, v, r, la, gain, s0, chunk):
    nb, length, gqk = q.shape
    gv = v.shape[2]
    dk, dv = gqk // GLA_HEADS, gv // GLA_HEADS
    mats = jnp.asarray(_gla_sum_matrices(chunk), dtype=BF16)
    n_levels = (mats.shape[0] // chunk - 2) // 2
    blk = lambda w: pl.BlockSpec((1, chunk, w), lambda b, c: (b, c, 0))
    st = pl.BlockSpec((1, GLA_HEADS, dk, dv), lambda b, c: (b, 0, 0, 0))
    return pl.pallas_call(
        functools.partial(_gla_kernel, chunk=chunk, n_levels=n_levels, dk=dk, dv=dv),
        grid=(nb, length // chunk),
        in_specs=[_full(mats.shape), blk(gqk), blk(gqk), blk(gv), blk(gv), blk(gqk), _full(gain.shape), st],
        out_specs=[blk(gv), st],
        out_shape=[jax.ShapeDtypeStruct((nb, length, gv), F32),
                   jax.ShapeDtypeStruct((nb, GLA_HEADS, dk, dv), F32)],
        scratch_shapes=[pltpu.VMEM((mats.shape[0], gqk), F32), pltpu.VMEM((GLA_HEADS, dk, dv), F32)],
        compiler_params=_params("parallel", "arbitrary"),
        name="gla",
    )(mats, q, k, v, r, la, gain, s0)


def _sb_block(z, valid, carry, tri_ref):
    lk = -_softplus(z)
    if valid is not None:
        lk = jnp.where(valid, lk, 0.0)
    hi, lo = _split_bf16(lk)
    tri = tri_ref[...]
    cs = _dot(hi, tri) + _dot(lo, tri)
    a = jnp.exp(z + cs + carry)
    if valid is not None:
        a = jnp.where(valid, a, 0.0)
    return a.astype(BF16), carry + cs[:, 0:1]


def _suffix_sum_matrix(n):
    j = np.arange(n)
    return jnp.asarray((j[:, None] >= j[None, :]).astype(np.float32), dtype=BF16)


def _sb_prompt_kernel(bias_ref, tri_ref, q_ref, k_ref, v_ref, o_ref, *, blk, dh):
    hp = pl.program_id(1)
    qi = pl.program_id(2)
    heads_per_step = q_ref.shape[2] // dh
    rows = lax.broadcasted_iota(jnp.int32, (blk, blk), 0)
    cols = lax.broadcasted_iota(jnp.int32, (blk, blk), 1)
    strictly_before = cols < rows
    for hh in range(heads_per_step):
        ls = slice(hh * dh, (hh + 1) * dh)
        bias = bias_ref[hp * heads_per_step + hh]
        q = q_ref[0][:, ls]

        def block(j, carry, acc, valid):
            start = pl.multiple_of(j * blk, blk)
            k = k_ref[0, pl.ds(start, blk), ls]
            v = v_ref[0, pl.ds(start, blk), ls]
            a, carry = _sb_block(_dot_nt(q, k) + bias, valid, carry, tri_ref)
            return carry, acc + _dot(a, v)

        carry, acc = block(qi, jnp.zeros((blk, 1), F32), jnp.zeros((blk, dh), F32), strictly_before)

        def body(it, state):
            return block(qi - 1 - it, state[0], state[1], None)

        carry, acc = lax.fori_loop(0, qi, body, (carry, acc))
        o_ref[0, :, ls] = acc


def _sb_prompt(q, k, v, bias, blk):
    bsz, length, w = q.shape
    dh = w // SB_HEADS
    tri = _suffix_sum_matrix(blk)
    qspec = pl.BlockSpec((1, blk, LANES), lambda b, h, i, *_: (b, i, h))
    kvspec = pl.BlockSpec((1, length, LANES), lambda b, h, i, *_: (b, 0, h))
    return pl.pallas_call(
        functools.partial(_sb_prompt_kernel, blk=blk, dh=dh),
        grid_spec=pltpu.PrefetchScalarGridSpec(
            num_scalar_prefetch=1,
            grid=(bsz, w // LANES, length // blk),
            in_specs=[pl.BlockSpec(tri.shape, lambda *_: (0, 0)), qspec, kvspec, kvspec],
            out_specs=qspec,
        ),
        out_shape=jax.ShapeDtypeStruct((bsz, length, w), F32),
        compiler_params=_params("parallel", "parallel", "arbitrary"),
        name="sb_prompt",
    )(bias, tri, q, k, v)


def _sb_sample_kernel(pt_ref, tri_ref, q_ref, bias_ref, hm_ref, kn_ref, vn_ref, kc_ref, vc_ref, o_ref,
                      carry_ref, acc_ref, pad_ref, *, n_new):
    del pt_ref
    p = pl.program_id(1)
    rows_q = q_ref.shape[1]
    q = q_ref[0]
    bias = bias_ref[...]

    @pl.when(p == 0)
    def _():
        n_pad = kn_ref.shape[1]
        pad_ref[...] = jnp.zeros_like(pad_ref)
        pad_ref[0:n_pad, :] = kn_ref[0].astype(BF16)
        z = _dot_nt(q, pad_ref[...]) + bias
        s_idx = lax.broadcasted_iota(jnp.int32, z.shape, 1)
        t_idx = lax.broadcasted_iota(jnp.int32, z.shape, 0) // SB_HEADS
        a, carry = _sb_block(z, s_idx < t_idx, jnp.zeros((rows_q, 1), F32), tri_ref)
        pad_ref[0:n_pad, :] = vn_ref[0].astype(BF16)
        carry_ref[...] = jnp.broadcast_to(carry, carry_ref.shape)
        acc_ref[...] = _dot(a, pad_ref[...])

    z = _dot(q, kc_ref[0].astype(BF16)) + bias
    a, carry = _sb_block(z, None, carry_ref[:, 0:1], tri_ref)
    carry_ref[...] = jnp.broadcast_to(carry, carry_ref.shape)
    acc_ref[...] += _dot_nt(a, vc_ref[0].astype(BF16))

    @pl.when(p == pl.num_programs(1) - 1)
    def _():
        acc = acc_ref[...].reshape(n_new, SB_HEADS, acc_ref.shape[1])
        o_ref[0] = jnp.sum(acc * hm_ref[...][None], axis=1)


def _sb_sample(q, k_new, v_new, cache_k, cache_v, page_table, bias):
    n, t_new, w = q.shape
    dh = w // SB_HEADS
    n_pages = page_table.shape[1]
    page = cache_k.shape[2]
    head_mask = (jnp.arange(w)[None, :] // dh == jnp.arange(SB_HEADS)[:, None]).astype(F32)
    q_rows = (q[:, :, None, :] * head_mask[None, None]).reshape(n, t_new * SB_HEADS, w).astype(BF16)
    bias_rows = jnp.broadcast_to(jnp.tile(bias, t_new)[:, None], (t_new * SB_HEADS, page)).astype(F32)
    n_pad = 8
    k_new = jnp.pad(k_new, ((0, 0), (0, n_pad - t_new), (0, 0)))
    v_new = jnp.pad(v_new, ((0, 0), (0, n_pad - t_new), (0, 0)))
    tri = _suffix_sum_matrix(page)
    rows_q = t_new * SB_HEADS
    per_seq = lambda r: pl.BlockSpec((1, r, w), lambda i, p, pt: (i, 0, 0))
    cache_spec = pl.BlockSpec((1, w, page), lambda i, p, pt: (pt[i, n_pages - 1 - p], 0, 0))
    return pl.pallas_call(
        functools.partial(_sb_sample_kernel, n_new=t_new),
        grid_spec=pltpu.PrefetchScalarGridSpec(
            num_scalar_prefetch=1,
            grid=(n, n_pages),
            in_specs=[pl.BlockSpec(tri.shape, lambda *_: (0, 0)), per_seq(rows_q),
                      pl.BlockSpec(bias_rows.shape, lambda *_: (0, 0)),
                      pl.BlockSpec(head_mask.shape, lambda *_: (0, 0)),
                      per_seq(n_pad), per_seq(n_pad), cache_spec, cache_spec],
            out_specs=per_seq(t_new),
            scratch_shapes=[pltpu.VMEM((rows_q, LANES), F32), pltpu.VMEM((rows_q, w), F32),
                            pltpu.VMEM((page, w), BF16)],
        ),
        out_shape=jax.ShapeDtypeStruct((n, t_new, w), F32),
        compiler_params=_params("parallel", "arbitrary"),
        name="sb_sample",
    )(page_table, tri, q_rows, bias_rows, head_mask, k_new, v_new, cache_k, cache_v)


def _merge_ffn_kernel(x_ref, oa_ref, ob_ref, wga_ref, wgb_ref, wo_ref, ln1g_ref, ln1b_ref,
                      wup_ref, wdn_ref, ln2g_ref, ln2b_ref, y_ref, h_ref, h16_ref, acc_ref):
    j = pl.program_id(1)

    @pl.when(j == 0)
    def _():
        x = x_ref[...]
        x16 = x.astype(BF16)
        mix_in = (_sigmoid(_dot(x16, wga_ref[...])) * oa_ref[...]
                  + _sigmoid(_dot(x16, wgb_ref[...])) * ob_ref[...])
        mix = _dot(mix_in.astype(BF16), wo_ref[...])
        h = _layer_norm(DN_ALPHA * x + mix, ln1g_ref[...], ln1b_ref[...])
        h_ref[...] = h
        h16_ref[...] = h.astype(BF16)
        acc_ref[...] = jnp.zeros_like(acc_ref)

    u = jnp.maximum(_dot(h16_ref[...], wup_ref[...]), 0.0)
    acc_ref[...] += _dot((u * u).astype(BF16), wdn_ref[...])

    @pl.when(j == pl.num_programs(1) - 1)
    def _():
        y_ref[...] = _layer_norm(DN_ALPHA * h_ref[...] + acc_ref[...], ln2g_ref[...], ln2b_ref[...])


def _merge_ffn(x, oa, ob, o_row_map, wga, wgb, wo, ln1g, ln1b, wup, wdn, ln2g, ln2b, tm, tf):
    n, d = x.shape
    dff = wup.shape[1]
    xrow = pl.BlockSpec((tm, d), lambda i, j: (i, 0))
    orow = pl.BlockSpec((tm, d), lambda i, j: (o_row_map(i), 0))
    const = lambda a: pl.BlockSpec(a.shape, lambda i, j: (0,) * a.ndim)
    return pl.pallas_call(
        _merge_ffn_kernel,
        grid=(n // tm, dff // tf),
        in_specs=[xrow, orow, orow, const(wga), const(wgb), const(wo), const(ln1g), const(ln1b),
                  pl.BlockSpec((d, tf), lambda i, j: (0, j)), pl.BlockSpec((tf, d), lambda i, j: (j, 0)),
                  const(ln2g), const(ln2b)],
        out_specs=xrow,
        out_shape=jax.ShapeDtypeStruct((n, d), F32),
        scratch_shapes=[pltpu.VMEM((tm, d), F32), pltpu.VMEM((tm, d), BF16), pltpu.VMEM((tm, d), F32)],
        compiler_params=_params("parallel", "arbitrary"),
        name="merge_ffn",
    )(x, oa, ob, wga, wgb, wo, ln1g, ln1b, wup, wdn, ln2g, ln2b)


def _row_tile(n, limit=512):
    t = limit
    while n % t:
        t //= 2
    return t


def kernel(x_prompt, x_sample, cache_sb_k, cache_sb_v, state_gla, page_table, meta_tokens, w_in,
           w_gate_a2, b_gate_a, gla_norm_g, sb_logit_bias, w_o, ln1_g, ln1_b, w_up, w_down, ln2_g, ln2_b):
    bsz, seq, d = x_prompt.shape
    n_dec, t_dec, _ = x_sample.shape
    _, _, n_heads_gla, dk, dv = state_gla.shape
    n_phys, page = cache_sb_k.shape[1], cache_sb_k.shape[2]
    gqk, gv = n_heads_gla * dk, n_heads_gla * dv
    sbw = cache_sb_k.shape[3] * cache_sb_k.shape[4]
    assert n_heads_gla == GLA_HEADS and cache_sb_k.shape[3] == SB_HEADS and w_in.shape[0] == DEPTH == 1
    assert page == PAGE_SIZE and w_gate_a2.shape[1] == GLA_RANK

    w = w_in[0]
    offs = np.cumsum([0, gqk, gqk, gv, gv, GLA_RANK, sbw, sbw, sbw, d, d])
    cols = lambda i: w[:, offs[i]:offs[i + 1]].astype(BF16)
    wq_a, wk_a, wv_a, wr_a, wa_lr, wq_b, wk_b, wv_b, wg_a, wg_b = (cols(i) for i in range(10))
    wa_lr = jnp.pad(wa_lr, ((0, 0), (0, LANES - GLA_RANK)))
    wg2 = jnp.pad(w_gate_a2[0].astype(BF16), ((0, LANES - GLA_RANK), (0, 0)))
    bg = b_gate_a[0][None].astype(F32)
    gain = gla_norm_g[0][None].astype(F32)
    bias = sb_logit_bias[0].astype(F32)
    wo16, wup16, wdn16 = w_o[0].astype(BF16), w_up[0].astype(BF16), w_down[0].astype(BF16)
    row = lambda a: a[0][None].astype(F32)
    ffn_w = (wg_a, wg_b, wo16, row(ln1_g), row(ln1_b), wup16, wdn16, row(ln2_g), row(ln2_b))
    tf = min(1024, wup16.shape[1])

    length = N_META + seq
    lp = -(-length // ROW_ALIGN) * ROW_ALIGN
    front = lp - length
    assert (front + N_META) % ROW_ALIGN == 0 and lp % GLA_CHUNK == 0 and lp % SB_BLOCK == 0
    meta = jnp.broadcast_to(meta_tokens.astype(BF16)[None], (bsz, N_META, d))
    xp = jnp.concatenate([jnp.zeros((bsz, front, d), BF16), meta, x_prompt.astype(BF16)], axis=1)
    xp = xp.reshape(bsz * lp, d)
    tm = _row_tile(bsz * lp)
    q_a, k_a, v_a, r_a, la = _proj_gla(xp, wq_a, wk_a, wv_a, wr_a, wa_lr, wg2, bg, tm)
    q_b, k_b, v_b, k_b16, v_b16 = _proj_sb(xp, wq_b, wk_b, wv_b, tm)
    seq3 = lambda a: a.reshape(bsz, lp, a.shape[-1])
    zero_state = jnp.zeros((bsz, GLA_HEADS, dk, dv), F32)
    o_a, s_p = _gla(seq3(q_a), seq3(k_a), seq3(v_a), seq3(r_a), seq3(la), gain, zero_state, GLA_CHUNK)
    o_b = _sb_prompt(seq3(q_b), seq3(k_b16), seq3(v_b16), bias, SB_BLOCK)
    tmf = ROW_ALIGN
    tiles_per_seq, skip = seq // tmf, (front + N_META) // tmf
    o_map = lambda i: (i // tiles_per_seq) * (lp // tmf) + skip + i % tiles_per_seq
    y_prompt = _merge_ffn(x_prompt.reshape(bsz * seq, d), o_a.reshape(bsz * lp, gv), o_b.reshape(bsz * lp, sbw),
                          o_map, *ffn_w, tmf, tf).reshape(bsz, seq, d)
    new_k_prompt = seq3(k_b)[:, front:].reshape(1, bsz, length, SB_HEADS, sbw // SB_HEADS)
    new_v_prompt = seq3(v_b)[:, front:].reshape(1, bsz, length, SB_HEADS, sbw // SB_HEADS)
    new_gla_prompt = s_p[None].astype(state_gla.dtype)

    ns = n_dec * t_dec
    xs = x_sample.reshape(ns, d)
    tms = _row_tile(ns)
    q_a, k_a, v_a, r_a, la = _proj_gla(xs, wq_a, wk_a, wv_a, wr_a, wa_lr, wg2, bg, tms)
    q_b, k_b, v_b, _, _ = _proj_sb(xs, wq_b, wk_b, wv_b, tms)
    pad3 = lambda a: jnp.pad(a.reshape(n_dec, t_dec, a.shape[-1]), ((0, 0), (0, SAMPLE_PAD - t_dec), (0, 0)))
    o_a, s_s = _gla(pad3(q_a), pad3(k_a), pad3(v_a), pad3(r_a), pad3(la), gain,
                    state_gla[0].astype(F32), SAMPLE_PAD)
    o_a = o_a[:, :t_dec].reshape(ns, gv)
    dec3 = lambda a: a.reshape(n_dec, t_dec, a.shape[-1])
    cache_t = lambda c: jnp.transpose(c[0], (0, 2, 3, 1)).reshape(n_phys, sbw, page)
    o_b = _sb_sample(dec3(q_b).astype(F32), dec3(k_b), dec3(v_b), cache_t(cache_sb_k), cache_t(cache_sb_v),
                     page_table, bias).reshape(ns, sbw)
    y_sample = _merge_ffn(xs, o_a, o_b, lambda i: i, *ffn_w, tms, tf).reshape(n_dec, t_dec, d)
    new_k_sample = k_b.reshape(1, n_dec, t_dec, SB_HEADS, sbw // SB_HEADS)
    new_v_sample = v_b.reshape(1, n_dec, t_dec, SB_HEADS, sbw // SB_HEADS)
    new_gla_sample = s_s[None].astype(state_gla.dtype)

    return (y_prompt, y_sample, new_k_prompt, new_v_prompt, new_gla_prompt,
            new_k_sample, new_v_sample, new_gla_sample)
```

```python
import functools

import numpy as np
import jax
import jax.numpy as jnp
from jax import lax
from jax.experimental import pallas as pl
from jax.experimental.pallas import tpu as pltpu

F32 = jnp.float32
BF16 = jnp.bfloat16

N_META = 16
GLA_HEADS = 4
GLA_RANK = 16
GLA_TAU = 16.0
SB_HEADS = 16
PAGE_SIZE = 128
LN_EPS = 1e-5
DEPTH = 1
DN_ALPHA = (2.0 * DEPTH) ** 0.25
LOG2E = 1.4426950408889634

LANES = 128
MXU_DIM = 256
VMEM_LIMIT_BYTES = 56 * 1024 * 1024

GLA_CHUNK = 128
SB_BLOCK = 256
ROW_ALIGN = 256
SAMPLE_PAD = 16


def _dot(a, b):
    return jnp.dot(a, b, preferred_element_type=F32)


def _dot_nt(a, b):
    return lax.dot_general(a, b, (((1,), (1,)), ((), ())), preferred_element_type=F32)


def _split_bf16(x):
    hi = x.astype(BF16)
    lo = (x - hi.astype(F32)).astype(BF16)
    return hi, lo


def _softplus(z):
    return jnp.maximum(z, 0.0) + jnp.log(1.0 + jnp.exp(-jnp.abs(z)))


def _sigmoid(z):
    return 1.0 / (1.0 + jnp.exp(-z))


def _layer_norm(x, g, b):
    mu = jnp.mean(x, axis=-1, keepdims=True)
    xc = x - mu
    var = jnp.mean(xc * xc, axis=-1, keepdims=True)
    return xc * lax.rsqrt(var + LN_EPS) * g + b


def _params(*sem):
    return pltpu.CompilerParams(dimension_semantics=sem, vmem_limit_bytes=VMEM_LIMIT_BYTES)


def _full(shape):
    return pl.BlockSpec(shape, lambda *_: (0,) * len(shape))


def _proj_gla_kernel(x_ref, wq_ref, wk_ref, wv_ref, wr_ref, wa_ref, wg_ref, bg_ref,
                     q_ref, k_ref, v_ref, r_ref, la_ref, *, q_scale):
    x = x_ref[...].astype(BF16)
    q_ref[...] = _dot(x, wq_ref[...]) * q_scale
    k_ref[...] = _dot(x, wk_ref[...])
    v_ref[...] = _dot(x, wv_ref[...]).astype(BF16)
    r = _dot(x, wr_ref[...])
    r_ref[...] = r * _sigmoid(r)
    a = _dot(x, wa_ref[...]).astype(BF16)
    zg = _dot(a, wg_ref[...]) + bg_ref[...]
    la_ref[...] = -_softplus(-zg) * (1.0 / GLA_TAU)


def _proj_gla(x, wq, wk, wv, wr, wa, wg, bg, tm):
    n, d = x.shape
    gqk, gv = wq.shape[1], wv.shape[1]
    dk = gqk // GLA_HEADS
    row = lambda c: pl.BlockSpec((tm, c), lambda i: (i, 0))
    return pl.pallas_call(
        functools.partial(_proj_gla_kernel, q_scale=dk ** -0.5),
        grid=(n // tm,),
        in_specs=[row(d), _full(wq.shape), _full(wk.shape), _full(wv.shape), _full(wr.shape),
                  _full(wa.shape), _full(wg.shape), _full(bg.shape)],
        out_specs=[row(gqk), row(gqk), row(gv), row(gv), row(gqk)],
        out_shape=[jax.ShapeDtypeStruct((n, gqk), F32), jax.ShapeDtypeStruct((n, gqk), F32),
                   jax.ShapeDtypeStruct((n, gv), BF16), jax.ShapeDtypeStruct((n, gv), F32),
                   jax.ShapeDtypeStruct((n, gqk), F32)],
        compiler_params=_params("parallel"),
        name="proj_gla",
    )(x, wq, wk, wv, wr, wa, wg, bg)


def _proj_sb_kernel(x_ref, wq_ref, wk_ref, wv_ref, q_ref, k_ref, v_ref, k16_ref, v16_ref, *, q_scale):
    x = x_ref[...].astype(BF16)
    q_ref[...] = (_dot(x, wq_ref[...]) * q_scale).astype(BF16)
    k = _dot(x, wk_ref[...])
    k_ref[...] = k
    k16_ref[...] = k.astype(BF16)
    v = _dot(x, wv_ref[...])
    v_ref[...] = v
    v16_ref[...] = v.astype(BF16)


def _proj_sb(x, wq, wk, wv, tm):
    n, d = x.shape
    w = wq.shape[1]
    dh = w // SB_HEADS
    row = pl.BlockSpec((tm, w), lambda i: (i, 0))
    return pl.pallas_call(
        functools.partial(_proj_sb_kernel, q_scale=dh ** -0.5 * LOG2E),
        grid=(n // tm,),
        in_specs=[pl.BlockSpec((tm, d), lambda i: (i, 0)), _full(wq.shape), _full(wk.shape), _full(wv.shape)],
        out_specs=[row, row, row, row, row],
        out_shape=[jax.ShapeDtypeStruct((n, w), BF16), jax.ShapeDtypeStruct((n, w), F32),
                   jax.ShapeDtypeStruct((n, w), F32), jax.ShapeDtypeStruct((n, w), BF16),
                   jax.ShapeDtypeStruct((n, w), BF16)],
        compiler_params=_params("parallel"),
        name="proj_sb",
    )(x, wq, wk, wv)


def _gla_sum_matrices(c):
    t = np.arange(c)
    j = t[None, :]
    mats = [j <= t[:, None], j > t[:, None]]
    l = 1
    while l < c:
        start = (t // l) * l
        mats.append((j > start[:, None]) & (j <= t[:, None]))
        nxt = np.minimum((t // l + 1) * l, c - 1)
        mats.append((j > t[:, None]) & (j <= nxt[:, None]))
        l *= 2
    return np.concatenate(mats, axis=0).astype(np.float32)


def _gla_kernel(m_ref, q_ref, k_ref, v_ref, r_ref, la_ref, g_ref, s0_ref, o_ref, s_out_ref,
                e_ref, s_ref, *, chunk, n_levels, dk, dv):
    c = pl.program_id(1)

    @pl.when(c == 0)
    def _():
        s_ref[...] = s0_ref[0]

    la_hi, la_lo = _split_bf16(la_ref[0])
    m = m_ref[...]
    e_ref[...] = _dot(m, la_hi) + _dot(m, la_lo)

    rows = lax.broadcasted_iota(jnp.int32, (chunk, chunk), 0)
    cols = lax.broadcasted_iota(jnp.int32, (chunk, chunk), 1)
    level = 31 - lax.clz(rows ^ cols)
    level = jnp.where(cols < rows, level, -1)
    diag = rows == cols

    for h in range(GLA_HEADS):
        ks = slice(h * dk, (h + 1) * dk)
        vs = slice(h * dv, (h + 1) * dv)
        q = q_ref[0][:, ks]
        k = k_ref[0][:, ks]
        v = v_ref[0][:, vs]
        b = e_ref[0:chunk, ks]
        tail = e_ref[chunk:2 * chunk, ks]
        scores = jnp.where(diag, _dot_nt(q.astype(BF16), k.astype(BF16)), 0.0)
        for li in range(n_levels):
            eq = e_ref[(2 + 2 * li) * chunk:(3 + 2 * li) * chunk, ks]
            ek = e_ref[(3 + 2 * li) * chunk:(4 + 2 * li) * chunk, ks]
            r = _dot_nt((q * jnp.exp(eq)).astype(BF16), (k * jnp.exp(ek)).astype(BF16))
            scores = jnp.where(level == li, r, scores)
        s_prev = s_ref[h]
        o = _dot(scores.astype(BF16), v) + _dot((q * jnp.exp(b)).astype(BF16), s_prev.astype(BF16))
        k_out_t = (k * jnp.exp(tail)).T.astype(BF16)
        decay_col = jnp.exp(b.T[:, chunk - 1:chunk])
        s_ref[h] = decay_col * s_prev + _dot(k_out_t, v)
        o = o * lax.rsqrt(jnp.mean(o * o, axis=-1, keepdims=True) + LN_EPS)
        o_ref[0, :, vs] = o * g_ref[:, vs] * r_ref[0][:, vs]

    @pl.when(c == pl.num_programs(1) - 1)
    def _():
        s_out_ref[0] = s_ref[...]


def _gla(q, k, v, r, la, gain, s0, chunk):
    nb, length, gqk = q.shape
    gv = v.shape[2]
    dk, dv = gqk // GLA_HEADS, gv // GLA_HEADS
    mats = jnp.asarray(_gla_sum_matrices(chunk), dtype=BF16)
    n_levels = (mats.shape[0] // chunk - 2) // 2
    blk = lambda w: pl.BlockSpec((1, chunk, w), lambda b, c: (b, c, 0))
    st = pl.BlockSpec((1, GLA_HEADS, dk, dv), lambda b, c: (b, 0, 0, 0))
    return pl.pallas_call(
        functools.partial(_gla_kernel, chunk=chunk, n_levels=n_levels, dk=dk, dv=dv),
        grid=(nb, length // chunk),
        in_specs=[_full(mats.shape), blk(gqk), blk(gqk), blk(gv), blk(gv), blk(gqk), _full(gain.shape), st],
        out_specs=[blk(gv), st],
        out_shape=[jax.ShapeDtypeStruct((nb, length, gv), F32),
                   jax.ShapeDtypeStruct((nb, GLA_HEADS, dk, dv), F32)],
        scratch_shapes=[pltpu.VMEM((mats.shape[0], gqk), F32), pltpu.VMEM((GLA_HEADS, dk, dv), F32)],
        compiler_params=_params("parallel", "arbitrary"),
        name="gla",
    )(mats, q, k, v, r, la, gain, s0)


def _neg_log2_keep(y):
    return jnp.maximum(y, 0.0) + jnp.log2(1.0 + jnp.exp2(-jnp.abs(y)))


def _neg_suffix_sum_matrix(n):
    j = np.arange(n)
    return jnp.asarray(-(j[:, None] >= j[None, :]).astype(np.float32), dtype=BF16)


def _sb_prompt_kernel(bias_ref, ntri_ref, q_ref, k_ref, v_ref, o_ref, acc_ref, *, blk, dh, unroll):
    hp = pl.program_id(1)
    qi = pl.program_id(2) + 1
    lane = lax.broadcasted_iota(jnp.int32, (blk, LANES), 1)
    q2 = q_ref[0]
    zero = jnp.zeros_like(q2)
    q = jnp.concatenate([jnp.where(lane < dh, q2, zero), jnp.where(lane >= dh, q2, zero)], axis=0)
    b0 = bias_ref[2 * hp]
    b1 = bias_ref[2 * hp + 1]
    rows = lax.broadcasted_iota(jnp.int32, (2 * blk, blk), 0)
    cols = lax.broadcasted_iota(jnp.int32, (2 * blk, blk), 1)
    strictly_before = cols < (rows & (blk - 1))

    def logits(j):
        start = pl.multiple_of(j * blk, blk)
        y = _dot_nt(q, k_ref[0, pl.ds(start, blk), :])
        return jnp.concatenate([y[:blk] + b0, y[blk:] + b1], axis=0)

    def suffix(y, valid):
        sp = _neg_log2_keep(y)
        if valid is not None:
            sp = jnp.where(valid, sp, 0.0)
        return _dot(sp.astype(BF16), ntri_ref[...])

    def weighted_values(j, y, cs, carry, valid):
        a = jnp.exp2(y + cs + carry)
        if valid is not None:
            a = jnp.where(valid, a, 0.0)
        start = pl.multiple_of(j * blk, blk)
        return _dot(a.astype(BF16), v_ref[0, pl.ds(start, blk), :])

    y = logits(qi)
    cs = suffix(y, strictly_before)
    acc_ref[...] = weighted_values(qi, y, cs, 0.0, strictly_before)
    carry = cs[:, 0:1]

    def make_body(n, first):
        def body(it, carry):
            js = [first - it * n - u for u in range(n)]
            ys = [logits(j) for j in js]
            css = [suffix(y, None) for y in ys]
            for j, y, cs in zip(js, ys, css):
                acc_ref[...] += weighted_values(j, y, cs, carry, None)
                carry = carry + cs[:, 0:1]
            return carry
        return body

    remaining, n = qi, unroll
    while n >= 1:
        trips = remaining // n
        carry = lax.fori_loop(0, trips, make_body(n, remaining - 1), carry)
        remaining = remaining - trips * n
        n //= 2
    acc = acc_ref[...]
    o_ref[0] = jnp.where(lane < dh, acc[:blk], acc[blk:])


def _sb_prompt(q, k, v, bias, blk, unroll=4):
    bsz, length, w = q.shape
    dh = w // SB_HEADS
    assert LANES == 2 * dh and blk & (blk - 1) == 0
    ntri = _neg_suffix_sum_matrix(blk)
    kvspec = pl.BlockSpec((1, length, LANES), lambda b, h, i, *_: (b, 0, h))
    return pl.pallas_call(
        functools.partial(_sb_prompt_kernel, blk=blk, dh=dh, unroll=unroll),
        grid_spec=pltpu.PrefetchScalarGridSpec(
            num_scalar_prefetch=1,
            grid=(bsz, w // LANES, length // blk - 1),
            in_specs=[pl.BlockSpec(ntri.shape, lambda *_: (0, 0)),
                      pl.BlockSpec((1, blk, LANES), lambda b, h, i, *_: (b, i + 1, h)), kvspec, kvspec],
            out_specs=pl.BlockSpec((1, blk, LANES), lambda b, h, i, *_: (b, i, h)),
            scratch_shapes=[pltpu.VMEM((2 * blk, LANES), F32)],
        ),
        out_shape=jax.ShapeDtypeStruct((bsz, length - blk, w), F32),
        compiler_params=_params("parallel", "parallel", "arbitrary"),
        name="sb_prompt",
    )(bias, ntri, q, k, v)


def _sb_sample_kernel(pt_ref, ntri_ref, q_ref, bias_ref, hm_ref, kn_ref, vn_ref, *rest, n_new, group, page):
    del pt_ref
    kc_refs, vc_refs = rest[:group], rest[group:2 * group]
    o_ref, carry_ref, acc_ref, pad_ref = rest[2 * group:]
    p = pl.program_id(1)
    rows_q = q_ref.shape[1]
    q = q_ref[0]
    bias = bias_ref[...]
    blk = ntri_ref.shape[0]
    per_blk = blk // page
    n_blk = group // per_blk

    @pl.when(p == 0)
    def _():
        n_pad = kn_ref.shape[1]
        pad_ref[...] = jnp.zeros_like(pad_ref)
        pad_ref[0:n_pad, :] = kn_ref[0].astype(BF16)
        y = _dot_nt(q, pad_ref[...]) + bias
        s_idx = lax.broadcasted_iota(jnp.int32, y.shape, 1)
        t_idx = lax.broadcasted_iota(jnp.int32, y.shape, 0) // SB_HEADS
        valid = s_idx < t_idx
        sp = jnp.where(valid, _neg_log2_keep(y), 0.0)
        cs = _dot(sp.astype(BF16), ntri_ref[0:page, 0:page])
        a = jnp.where(valid, jnp.exp2(y + cs), 0.0)
        pad_ref[0:n_pad, :] = vn_ref[0].astype(BF16)
        carry_ref[...] = jnp.broadcast_to(cs[:, 0:1], carry_ref.shape)
        acc_ref[...] = _dot(a.astype(BF16), pad_ref[...])

    k_cat = jnp.concatenate([r[0].astype(BF16) for r in kc_refs], axis=1)
    y = _dot(q, k_cat) + jnp.concatenate([bias] * group, axis=1)
    sp = _neg_log2_keep(y).astype(BF16)
    sp_rows = jnp.concatenate([sp[:, b * blk:(b + 1) * blk] for b in range(n_blk)], axis=0)
    cs_rows = _dot(sp_rows, ntri_ref[...])
    carry = carry_ref[:, 0:1]
    a_blocks = [None] * n_blk
    for b in reversed(range(n_blk)):
        cs = cs_rows[b * rows_q:(b + 1) * rows_q]
        a_blocks[b] = jnp.exp2(y[:, b * blk:(b + 1) * blk] + cs + carry).astype(BF16)
        carry = carry + cs[:, 0:1]
    carry_ref[...] = jnp.broadcast_to(carry, carry_ref.shape)
    v_cat = jnp.concatenate([r[0].astype(BF16) for r in vc_refs], axis=1)
    acc_ref[...] += _dot_nt(jnp.concatenate(a_blocks, axis=1), v_cat)

    @pl.when(p == pl.num_programs(1) - 1)
    def _():
        acc = acc_ref[...].reshape(n_new, SB_HEADS, acc_ref.shape[1])
        o_ref[0] = jnp.sum(acc * hm_ref[...][None], axis=1)


def _sb_sample(q, k_new, v_new, cache_k, cache_v, page_table, bias, group=8):
    n, t_new, w = q.shape
    dh = w // SB_HEADS
    n_pages = page_table.shape[1]
    page = cache_k.shape[2]
    blk = MXU_DIM
    assert n_pages % group == 0 and (group * page) % blk == 0 and blk % page == 0
    head_mask = (jnp.arange(w)[None, :] // dh == jnp.arange(SB_HEADS)[:, None]).astype(F32)
    q_rows = (q[:, :, None, :] * head_mask[None, None]).reshape(n, t_new * SB_HEADS, w).astype(BF16)
    bias_rows = jnp.broadcast_to(jnp.tile(bias, t_new)[:, None], (t_new * SB_HEADS, page)).astype(F32)
    n_pad = 8
    k_new = jnp.pad(k_new, ((0, 0), (0, n_pad - t_new), (0, 0)))
    v_new = jnp.pad(v_new, ((0, 0), (0, n_pad - t_new), (0, 0)))
    ntri = _neg_suffix_sum_matrix(blk)
    rows_q = t_new * SB_HEADS
    per_seq = lambda r: pl.BlockSpec((1, r, w), lambda i, p, pt: (i, 0, 0))
    cache_spec = lambda g: pl.BlockSpec(
        (1, w, page), lambda i, p, pt: (pt[i, n_pages - (p + 1) * group + g], 0, 0))
    cache_specs = [cache_spec(g) for g in range(group)]
    return pl.pallas_call(
        functools.partial(_sb_sample_kernel, n_new=t_new, group=group, page=page),
        grid_spec=pltpu.PrefetchScalarGridSpec(
            num_scalar_prefetch=1,
            grid=(n, n_pages // group),
            in_specs=[pl.BlockSpec(ntri.shape, lambda *_: (0, 0)), per_seq(rows_q),
                      pl.BlockSpec(bias_rows.shape, lambda *_: (0, 0)),
                      pl.BlockSpec(head_mask.shape, lambda *_: (0, 0)),
                      per_seq(n_pad), per_seq(n_pad)] + cache_specs + cache_specs,
            out_specs=per_seq(t_new),
            scratch_shapes=[pltpu.VMEM((rows_q, LANES), F32), pltpu.VMEM((rows_q, w), F32),
                            pltpu.VMEM((page, w), BF16)],
        ),
        out_shape=jax.ShapeDtypeStruct((n, t_new, w), F32),
        compiler_params=_params("parallel", "arbitrary"),
        name="sb_sample",
    )(page_table, ntri, q_rows, bias_rows, head_mask, k_new, v_new,
      *([cache_k] * group), *([cache_v] * group))


def _merge_ffn_kernel(x_ref, oa_ref, ob_ref, wga_ref, wgb_ref, wo_ref, ln1g_ref, ln1b_ref,
                      wup_ref, wdn_ref, ln2g_ref, ln2b_ref, y_ref, h_ref, h16_ref, acc_ref):
    j = pl.program_id(1)

    @pl.when(j == 0)
    def _():
        x = x_ref[...]
        x16 = x.astype(BF16)
        mix_in = (_sigmoid(_dot(x16, wga_ref[...])) * oa_ref[...]
                  + _sigmoid(_dot(x16, wgb_ref[...])) * ob_ref[...])
        mix = _dot(mix_in.astype(BF16), wo_ref[...])
        h = _layer_norm(DN_ALPHA * x + mix, ln1g_ref[...], ln1b_ref[...])
        h_ref[...] = h
        h16_ref[...] = h.astype(BF16)
        acc_ref[...] = jnp.zeros_like(acc_ref)

    u = jnp.maximum(_dot(h16_ref[...], wup_ref[...]), 0.0)
    acc_ref[...] += _dot((u * u).astype(BF16), wdn_ref[...])

    @pl.when(j == pl.num_programs(1) - 1)
    def _():
        y_ref[...] = _layer_norm(DN_ALPHA * h_ref[...] + acc_ref[...], ln2g_ref[...], ln2b_ref[...])


def _merge_ffn(x, oa, ob, oa_row_map, ob_row_map, wga, wgb, wo, ln1g, ln1b, wup, wdn, ln2g, ln2b, tm, tf):
    n, d = x.shape
    dff = wup.shape[1]
    xrow = pl.BlockSpec((tm, d), lambda i, j: (i, 0))
    oarow = pl.BlockSpec((tm, d), lambda i, j: (oa_row_map(i), 0))
    obrow = pl.BlockSpec((tm, d), lambda i, j: (ob_row_map(i), 0))
    const = lambda a: pl.BlockSpec(a.shape, lambda i, j: (0,) * a.ndim)
    return pl.pallas_call(
        _merge_ffn_kernel,
        grid=(n // tm, dff // tf),
        in_specs=[xrow, oarow, obrow, const(wga), const(wgb), const(wo), const(ln1g), const(ln1b),
                  pl.BlockSpec((d, tf), lambda i, j: (0, j)), pl.BlockSpec((tf, d), lambda i, j: (j, 0)),
                  const(ln2g), const(ln2b)],
        out_specs=xrow,
        out_shape=jax.ShapeDtypeStruct((n, d), F32),
        scratch_shapes=[pltpu.VMEM((tm, d), F32), pltpu.VMEM((tm, d), BF16), pltpu.VMEM((tm, d), F32)],
        compiler_params=_params("parallel", "arbitrary"),
        name="merge_ffn",
    )(x, oa, ob, wga, wgb, wo, ln1g, ln1b, wup, wdn, ln2g, ln2b)


def _row_tile(n, limit=512):
    t = limit
    while n % t:
        t //= 2
    return t


def kernel(x_prompt, x_sample, cache_sb_k, cache_sb_v, state_gla, page_table, meta_tokens, w_in,
           w_gate_a2, b_gate_a, gla_norm_g, sb_logit_bias, w_o, ln1_g, ln1_b, w_up, w_down, ln2_g, ln2_b):
    bsz, seq, d = x_prompt.shape
    n_dec, t_dec, _ = x_sample.shape
    _, _, n_heads_gla, dk, dv = state_gla.shape
    n_phys, page = cache_sb_k.shape[1], cache_sb_k.shape[2]
    gqk, gv = n_heads_gla * dk, n_heads_gla * dv
    sbw = cache_sb_k.shape[3] * cache_sb_k.shape[4]
    assert n_heads_gla == GLA_HEADS and cache_sb_k.shape[3] == SB_HEADS and w_in.shape[0] == DEPTH == 1
    assert page == PAGE_SIZE and w_gate_a2.shape[1] == GLA_RANK

    w = w_in[0]
    offs = np.cumsum([0, gqk, gqk, gv, gv, GLA_RANK, sbw, sbw, sbw, d, d])
    cols = lambda i: w[:, offs[i]:offs[i + 1]].astype(BF16)
    wq_a, wk_a, wv_a, wr_a, wa_lr, wq_b, wk_b, wv_b, wg_a, wg_b = (cols(i) for i in range(10))
    wa_lr = jnp.pad(wa_lr, ((0, 0), (0, LANES - GLA_RANK)))
    wg2 = jnp.pad(w_gate_a2[0].astype(BF16), ((0, LANES - GLA_RANK), (0, 0)))
    bg = b_gate_a[0][None].astype(F32)
    gain = gla_norm_g[0][None].astype(F32)
    bias = sb_logit_bias[0].astype(F32) * LOG2E
    wo16, wup16, wdn16 = w_o[0].astype(BF16), w_up[0].astype(BF16), w_down[0].astype(BF16)
    row = lambda a: a[0][None].astype(F32)
    ffn_w = (wg_a, wg_b, wo16, row(ln1_g), row(ln1_b), wup16, wdn16, row(ln2_g), row(ln2_b))
    tf = min(1024, wup16.shape[1])

    length = N_META + seq
    lp = -(-length // ROW_ALIGN) * ROW_ALIGN
    front = lp - length
    assert lp % GLA_CHUNK == 0 and lp % SB_BLOCK == 0
    meta = jnp.broadcast_to(meta_tokens.astype(BF16)[None], (bsz, N_META, d))
    xp = jnp.concatenate([jnp.zeros((bsz, front, d), BF16), meta, x_prompt.astype(BF16)], axis=1)
    xp = xp.reshape(bsz * lp, d)
    tm = _row_tile(bsz * lp)
    q_a, k_a, v_a, r_a, la = _proj_gla(xp, wq_a, wk_a, wv_a, wr_a, wa_lr, wg2, bg, tm)
    q_b, k_b, v_b, k_b16, v_b16 = _proj_sb(xp, wq_b, wk_b, wv_b, tm)
    seq3 = lambda a: a.reshape(bsz, lp, a.shape[-1])
    zero_state = jnp.zeros((bsz, GLA_HEADS, dk, dv), F32)
    o_a, s_p = _gla(seq3(q_a), seq3(k_a), seq3(v_a), seq3(r_a), seq3(la), gain, zero_state, GLA_CHUNK)
    o_b = _sb_prompt(seq3(q_b), seq3(k_b16), seq3(v_b16), bias, SB_BLOCK)
    tmf = ROW_ALIGN
    assert tmf == SB_BLOCK and front + N_META == tmf
    tiles_per_seq = seq // tmf
    oa_map = lambda i: (i // tiles_per_seq) * (lp // tmf) + 1 + i % tiles_per_seq
    y_prompt = _merge_ffn(x_prompt.reshape(bsz * seq, d), o_a.reshape(bsz * lp, gv), o_b.reshape(bsz * seq, sbw),
                          oa_map, lambda i: i, *ffn_w, tmf, tf).reshape(bsz, seq, d)
    new_k_prompt = seq3(k_b)[:, front:].reshape(1, bsz, length, SB_HEADS, sbw // SB_HEADS)
    new_v_prompt = seq3(v_b)[:, front:].reshape(1, bsz, length, SB_HEADS, sbw // SB_HEADS)
    new_gla_prompt = s_p[None].astype(state_gla.dtype)

    ns = n_dec * t_dec
    xs = x_sample.reshape(ns, d)
    tms = _row_tile(ns)
    q_a, k_a, v_a, r_a, la = _proj_gla(xs, wq_a, wk_a, wv_a, wr_a, wa_lr, wg2, bg, tms)
    q_b, k_b, v_b, _, _ = _proj_sb(xs, wq_b, wk_b, wv_b, tms)
    pad3 = lambda a: jnp.pad(a.reshape(n_dec, t_dec, a.shape[-1]), ((0, 0), (0, SAMPLE_PAD - t_dec), (0, 0)))
    o_a, s_s = _gla(pad3(q_a), pad3(k_a), pad3(v_a), pad3(r_a), pad3(la), gain,
                    state_gla[0].astype(F32), SAMPLE_PAD)
    o_a = o_a[:, :t_dec].reshape(ns, gv)
    dec3 = lambda a: a.reshape(n_dec, t_dec, a.shape[-1])
    cache_t = lambda c: jnp.transpose(c[0], (0, 2, 3, 1)).reshape(n_phys, sbw, page)
    o_b = _sb_sample(dec3(q_b).astype(F32), dec3(k_b), dec3(v_b), cache_t(cache_sb_k), cache_t(cache_sb_v),
                     page_table, bias).reshape(ns, sbw)
    y_sample = _merge_ffn(xs, o_a, o_b, lambda i: i, lambda i: i, *ffn_w, tms, tf).reshape(n_dec, t_dec, d)
    new_k_sample = k_b.reshape(1, n_dec, t_dec, SB_HEADS, sbw // SB_HEADS)
    new_v_sample = v_b.reshape(1, n_dec, t_dec, SB_HEADS, sbw // SB_HEADS)
    new_gla_sample = s_s[None].astype(state_gla.dtype)

    return (y_prompt, y_sample, new_k_prompt, new_v_prompt, new_gla_prompt,
            new_k_sample, new_v_sample, new_gla_sample)
```

```python
import functools

import numpy as np
import jax
import jax.numpy as jnp
from jax import lax
from jax.experimental import pallas as pl
from jax.experimental.pallas import tpu as pltpu

F32 = jnp.float32
BF16 = jnp.bfloat16

N_META = 16
GLA_HEADS = 4
GLA_RANK = 16
GLA_TAU = 16.0
SB_HEADS = 16
PAGE_SIZE = 128
LN_EPS = 1e-5
DEPTH = 1
DN_ALPHA = (2.0 * DEPTH) ** 0.25
LOG2E = 1.4426950408889634

LANES = 128
MXU_DIM = 256
VMEM_LIMIT_BYTES = 56 * 1024 * 1024

GLA_CHUNK = 128
SB_BLOCK = 256
ROW_ALIGN = 256
SAMPLE_PAD = 16


def _dot(a, b):
    return jnp.dot(a, b, preferred_element_type=F32)


def _dot_nt(a, b):
    return lax.dot_general(a, b, (((1,), (1,)), ((), ())), preferred_element_type=F32)


def _split_bf16(x):
    hi = x.astype(BF16)
    lo = (x - hi.astype(F32)).astype(BF16)
    return hi, lo


def _softplus(z):
    return jnp.maximum(z, 0.0) + jnp.log(1.0 + jnp.exp(-jnp.abs(z)))


def _sigmoid(z):
    return 1.0 / (1.0 + jnp.exp(-z))


def _layer_norm(x, g, b):
    mu = jnp.mean(x, axis=-1, keepdims=True)
    xc = x - mu
    var = jnp.mean(xc * xc, axis=-1, keepdims=True)
    return xc * lax.rsqrt(var + LN_EPS) * g + b


def _params(*sem):
    return pltpu.CompilerParams(dimension_semantics=sem, vmem_limit_bytes=VMEM_LIMIT_BYTES)


def _full(shape):
    return pl.BlockSpec(shape, lambda *_: (0,) * len(shape))


def _proj_gla_kernel(x_ref, wq_ref, wk_ref, wv_ref, wr_ref, wa_ref, wg_ref, bg_ref,
                     q_ref, k_ref, v_ref, r_ref, la_ref, *, q_scale):
    x = x_ref[...].astype(BF16)
    q_ref[...] = _dot(x, wq_ref[...]) * q_scale
    k_ref[...] = _dot(x, wk_ref[...])
    v_ref[...] = _dot(x, wv_ref[...]).astype(BF16)
    r = _dot(x, wr_ref[...])
    r_ref[...] = r * _sigmoid(r)
    a = _dot(x, wa_ref[...]).astype(BF16)
    zg = _dot(a, wg_ref[...]) + bg_ref[...]
    la_ref[...] = -_softplus(-zg) * (1.0 / GLA_TAU)


def _proj_gla(x, wq, wk, wv, wr, wa, wg, bg, tm):
    n, d = x.shape
    gqk, gv = wq.shape[1], wv.shape[1]
    dk = gqk // GLA_HEADS
    row = lambda c: pl.BlockSpec((tm, c), lambda i: (i, 0))
    return pl.pallas_call(
        functools.partial(_proj_gla_kernel, q_scale=dk ** -0.5),
        grid=(n // tm,),
        in_specs=[row(d), _full(wq.shape), _full(wk.shape), _full(wv.shape), _full(wr.shape),
                  _full(wa.shape), _full(wg.shape), _full(bg.shape)],
        out_specs=[row(gqk), row(gqk), row(gv), row(gv), row(gqk)],
        out_shape=[jax.ShapeDtypeStruct((n, gqk), F32), jax.ShapeDtypeStruct((n, gqk), F32),
                   jax.ShapeDtypeStruct((n, gv), BF16), jax.ShapeDtypeStruct((n, gv), F32),
                   jax.ShapeDtypeStruct((n, gqk), F32)],
        compiler_params=_params("parallel"),
        name="proj_gla",
    )(x, wq, wk, wv, wr, wa, wg, bg)


def _proj_sb_kernel(x_ref, wq_ref, wk_ref, wv_ref, q_ref, k_ref, v_ref, k16_ref, v16_ref, *, q_scale):
    x = x_ref[...].astype(BF16)
    q_ref[...] = (_dot(x, wq_ref[...]) * q_scale).astype(BF16)
    k = _dot(x, wk_ref[...])
    k_ref[...] = k
    k16_ref[...] = k.astype(BF16)
    v = _dot(x, wv_ref[...])
    v_ref[...] = v
    v16_ref[...] = v.astype(BF16)


def _proj_sb(x, wq, wk, wv, tm):
    n, d = x.shape
    w = wq.shape[1]
    dh = w // SB_HEADS
    row = pl.BlockSpec((tm, w), lambda i: (i, 0))
    return pl.pallas_call(
        functools.partial(_proj_sb_kernel, q_scale=dh ** -0.5 * LOG2E),
        grid=(n // tm,),
        in_specs=[pl.BlockSpec((tm, d), lambda i: (i, 0)), _full(wq.shape), _full(wk.shape), _full(wv.shape)],
        out_specs=[row, row, row, row, row],
        out_shape=[jax.ShapeDtypeStruct((n, w), BF16), jax.ShapeDtypeStruct((n, w), F32),
                   jax.ShapeDtypeStruct((n, w), F32), jax.ShapeDtypeStruct((n, w), BF16),
                   jax.ShapeDtypeStruct((n, w), BF16)],
        compiler_params=_params("parallel"),
        name="proj_sb",
    )(x, wq, wk, wv)


def _gla_sum_matrices(c):
    t = np.arange(c)
    j = t[None, :]
    mats = [j <= t[:, None], j > t[:, None]]
    l = 1
    while l < c:
        start = (t // l) * l
        mats.append((j > start[:, None]) & (j <= t[:, None]))
        nxt = np.minimum((t // l + 1) * l, c - 1)
        mats.append((j > t[:, None]) & (j <= nxt[:, None]))
        l *= 2
    return np.concatenate(mats, axis=0).astype(np.float32)


def _gla_kernel(m_ref, q_ref, k_ref, v_ref, r_ref, la_ref, g_ref, s0_ref, o_ref, s_out_ref,
                e_ref, s_ref, *, chunk, n_levels, dk, dv):
    c = pl.program_id(1)

    @pl.when(c == 0)
    def _():
        s_ref[...] = s0_ref[0]

    la_hi, la_lo = _split_bf16(la_ref[0])
    m = m_ref[...]
    e_ref[...] = _dot(m, la_hi) + _dot(m, la_lo)

    rows = lax.broadcasted_iota(jnp.int32, (chunk, chunk), 0)
    cols = lax.broadcasted_iota(jnp.int32, (chunk, chunk), 1)
    level = 31 - lax.clz(rows ^ cols)
    level = jnp.where(cols < rows, level, -1)
    diag = rows == cols

    for h in range(GLA_HEADS):
        ks = slice(h * dk, (h + 1) * dk)
        vs = slice(h * dv, (h + 1) * dv)
        q = q_ref[0][:, ks]
        k = k_ref[0][:, ks]
        v = v_ref[0][:, vs]
        b = e_ref[0:chunk, ks]
        tail = e_ref[chunk:2 * chunk, ks]
        scores = jnp.where(diag, _dot_nt(q.astype(BF16), k.astype(BF16)), 0.0)
        for li in range(n_levels):
            eq = e_ref[(2 + 2 * li) * chunk:(3 + 2 * li) * chunk, ks]
            ek = e_ref[(3 + 2 * li) * chunk:(4 + 2 * li) * chunk, ks]
            r = _dot_nt((q * jnp.exp(eq)).astype(BF16), (k * jnp.exp(ek)).astype(BF16))
            scores = jnp.where(level == li, r, scores)
        s_prev = s_ref[h]
        o = _dot(scores.astype(BF16), v) + _dot((q * jnp.exp(b)).astype(BF16), s_prev.astype(BF16))
        k_out_t = (k * jnp.exp(tail)).T.astype(BF16)
        decay_col = jnp.exp(b.T[:, chunk - 1:chunk])
        s_ref[h] = decay_col * s_prev + _dot(k_out_t, v)
        o = o * lax.rsqrt(jnp.mean(o * o, axis=-1, keepdims=True) + LN_EPS)
        o_ref[0, :, vs] = o * g_ref[:, vs] * r_ref[0][:, vs]

    @pl.when(c == pl.num_programs(1) - 1)
    def _():
        s_out_ref[0] = s_ref[...]


def _gla(q, k, v, r, la, gain, s0, chunk, skip=0):
    nb, length, gqk = q.shape
    gv = v.shape[2]
    dk, dv = gqk // GLA_HEADS, gv // GLA_HEADS
    mats = jnp.asarray(_gla_sum_matrices(chunk), dtype=BF16)
    n_levels = (mats.shape[0] // chunk - 2) // 2
    blk = lambda w: pl.BlockSpec((1, chunk, w), lambda b, c: (b, c, 0))
    st = pl.BlockSpec((1, GLA_HEADS, dk, dv), lambda b, c: (b, 0, 0, 0))
    return pl.pallas_call(
        functools.partial(_gla_kernel, chunk=chunk, n_levels=n_levels, dk=dk, dv=dv),
        grid=(nb, length // chunk),
        in_specs=[_full(mats.shape), blk(gqk), blk(gqk), blk(gv), blk(gv), blk(gqk), _full(gain.shape), st],
        out_specs=[pl.BlockSpec((1, chunk, gv), lambda b, c: (b, jnp.maximum(c - skip, 0), 0)), st],
        out_shape=[jax.ShapeDtypeStruct((nb, length - skip * chunk, gv), F32),
                   jax.ShapeDtypeStruct((nb, GLA_HEADS, dk, dv), F32)],
        scratch_shapes=[pltpu.VMEM((mats.shape[0], gqk), F32), pltpu.VMEM((GLA_HEADS, dk, dv), F32)],
        compiler_params=_params("parallel", "arbitrary"),
        name="gla",
    )(mats, q, k, v, r, la, gain, s0)


def _neg_log2_keep(y):
    return jnp.maximum(y, 0.0) + jnp.log2(1.0 + jnp.exp2(-jnp.abs(y)))


def _neg_suffix_sum_matrix(n):
    j = np.arange(n)
    return jnp.asarray(-(j[:, None] >= j[None, :]).astype(np.float32), dtype=BF16)


def _sb_prompt_kernel(bias_ref, ntri_ref, q_ref, k_ref, v_ref, o_ref, acc_ref, *, blk, dh, unroll):
    hp = pl.program_id(1)
    qi = pl.program_id(2) + 1
    lane = lax.broadcasted_iota(jnp.int32, (blk, LANES), 1)
    q2 = q_ref[0]
    zero = jnp.zeros_like(q2)
    q = jnp.concatenate([jnp.where(lane < dh, q2, zero), jnp.where(lane >= dh, q2, zero)], axis=0)
    b0 = bias_ref[2 * hp]
    b1 = bias_ref[2 * hp + 1]
    rows = lax.broadcasted_iota(jnp.int32, (2 * blk, blk), 0)
    cols = lax.broadcasted_iota(jnp.int32, (2 * blk, blk), 1)
    strictly_before = cols < (rows & (blk - 1))

    def logits(j):
        start = pl.multiple_of(j * blk, blk)
        y = _dot_nt(q, k_ref[0, pl.ds(start, blk), :])
        return jnp.concatenate([y[:blk] + b0, y[blk:] + b1], axis=0)

    def suffix(y, valid):
        sp = _neg_log2_keep(y)
        if valid is not None:
            sp = jnp.where(valid, sp, 0.0)
        return _dot(sp.astype(BF16), ntri_ref[...])

    def weighted_values(j, y, cs, carry, valid):
        a = jnp.exp2(y + cs + carry)
        if valid is not None:
            a = jnp.where(valid, a, 0.0)
        start = pl.multiple_of(j * blk, blk)
        return _dot(a.astype(BF16), v_ref[0, pl.ds(start, blk), :])

    y = logits(qi)
    cs = suffix(y, strictly_before)
    acc_ref[...] = weighted_values(qi, y, cs, 0.0, strictly_before)
    carry = cs[:, 0:1]

    def make_body(n, first):
        def body(it, carry):
            js = [first - it * n - u for u in range(n)]
            ys = [logits(j) for j in js]
            css = [suffix(y, None) for y in ys]
            for j, y, cs in zip(js, ys, css):
                acc_ref[...] += weighted_values(j, y, cs, carry, None)
                carry = carry + cs[:, 0:1]
            return carry
        return body

    remaining, n = qi, unroll
    while n >= 1:
        trips = remaining // n
        carry = lax.fori_loop(0, trips, make_body(n, remaining - 1), carry)
        remaining = remaining - trips * n
        n //= 2
    acc = acc_ref[...]
    o_ref[0] = jnp.where(lane < dh, acc[:blk], acc[blk:])


def _sb_prompt(q, k, v, bias, blk, unroll=8):
    bsz, length, w = q.shape
    dh = w // SB_HEADS
    assert LANES == 2 * dh and blk & (blk - 1) == 0
    ntri = _neg_suffix_sum_matrix(blk)
    kvspec = pl.BlockSpec((1, length, LANES), lambda b, h, i, *_: (b, 0, h))
    return pl.pallas_call(
        functools.partial(_sb_prompt_kernel, blk=blk, dh=dh, unroll=unroll),
        grid_spec=pltpu.PrefetchScalarGridSpec(
            num_scalar_prefetch=1,
            grid=(bsz, w // LANES, length // blk - 1),
            in_specs=[pl.BlockSpec(ntri.shape, lambda *_: (0, 0)),
                      pl.BlockSpec((1, blk, LANES), lambda b, h, i, *_: (b, i + 1, h)), kvspec, kvspec],
            out_specs=pl.BlockSpec((1, blk, LANES), lambda b, h, i, *_: (b, i, h)),
            scratch_shapes=[pltpu.VMEM((2 * blk, LANES), F32)],
        ),
        out_shape=jax.ShapeDtypeStruct((bsz, length - blk, w), F32),
        compiler_params=_params("parallel", "parallel", "arbitrary"),
        name="sb_prompt",
    )(bias, ntri, q, k, v)


def _sb_sample_kernel(pt_ref, ntri_ref, q_ref, bias_ref, hm_ref, kn_ref, vn_ref, *rest, n_new, group, page):
    del pt_ref
    kc_refs, vc_refs = rest[:group], rest[group:2 * group]
    o_ref, carry_ref, acc_ref, pad_ref = rest[2 * group:]
    p = pl.program_id(1)
    rows_q = q_ref.shape[1]
    q = q_ref[0]
    bias = bias_ref[...]
    blk = ntri_ref.shape[0]
    per_blk = blk // page
    n_blk = group // per_blk

    @pl.when(p == 0)
    def _():
        n_pad = kn_ref.shape[1]
        pad_ref[...] = jnp.zeros_like(pad_ref)
        pad_ref[0:n_pad, :] = kn_ref[0].astype(BF16)
        y = _dot_nt(q, pad_ref[...]) + bias
        s_idx = lax.broadcasted_iota(jnp.int32, y.shape, 1)
        t_idx = lax.broadcasted_iota(jnp.int32, y.shape, 0) // SB_HEADS
        valid = s_idx < t_idx
        sp = jnp.where(valid, _neg_log2_keep(y), 0.0)
        cs = _dot(sp.astype(BF16), ntri_ref[0:page, 0:page])
        a = jnp.where(valid, jnp.exp2(y + cs), 0.0)
        pad_ref[0:n_pad, :] = vn_ref[0].astype(BF16)
        carry_ref[...] = jnp.broadcast_to(cs[:, 0:1], carry_ref.shape)
        acc_ref[...] = _dot(a.astype(BF16), pad_ref[...])

    k_cat = jnp.concatenate([r[0].astype(BF16) for r in kc_refs], axis=1)
    y = _dot(q, k_cat) + jnp.concatenate([bias] * group, axis=1)
    sp = _neg_log2_keep(y).astype(BF16)
    sp_rows = jnp.concatenate([sp[:, b * blk:(b + 1) * blk] for b in range(n_blk)], axis=0)
    cs_rows = _dot(sp_rows, ntri_ref[...])
    carry = carry_ref[:, 0:1]
    a_blocks = [None] * n_blk
    for b in reversed(range(n_blk)):
        cs = cs_rows[b * rows_q:(b + 1) * rows_q]
        a_blocks[b] = jnp.exp2(y[:, b * blk:(b + 1) * blk] + cs + carry).astype(BF16)
        carry = carry + cs[:, 0:1]
    carry_ref[...] = jnp.broadcast_to(carry, carry_ref.shape)
    v_cat = jnp.concatenate([r[0].astype(BF16) for r in vc_refs], axis=1)
    acc_ref[...] += _dot_nt(jnp.concatenate(a_blocks, axis=1), v_cat)

    @pl.when(p == pl.num_programs(1) - 1)
    def _():
        acc = acc_ref[...].reshape(n_new, SB_HEADS, acc_ref.shape[1])
        o_ref[0] = jnp.sum(acc * hm_ref[...][None], axis=1)


def _sb_sample(q, k_new, v_new, cache_k, cache_v, page_table, bias, group=16):
    n, t_new, w = q.shape
    dh = w // SB_HEADS
    n_pages = page_table.shape[1]
    page = cache_k.shape[2]
    blk = MXU_DIM
    assert n_pages % group == 0 and (group * page) % blk == 0 and blk % page == 0
    head_mask = (jnp.arange(w)[None, :] // dh == jnp.arange(SB_HEADS)[:, None]).astype(F32)
    q_rows = (q[:, :, None, :] * head_mask[None, None]).reshape(n, t_new * SB_HEADS, w).astype(BF16)
    bias_rows = jnp.broadcast_to(jnp.tile(bias, t_new)[:, None], (t_new * SB_HEADS, page)).astype(F32)
    n_pad = 8
    k_new = jnp.pad(k_new, ((0, 0), (0, n_pad - t_new), (0, 0)))
    v_new = jnp.pad(v_new, ((0, 0), (0, n_pad - t_new), (0, 0)))
    ntri = _neg_suffix_sum_matrix(blk)
    rows_q = t_new * SB_HEADS
    per_seq = lambda r: pl.BlockSpec((1, r, w), lambda i, p, pt: (i, 0, 0))
    cache_spec = lambda g: pl.BlockSpec(
        (1, w, page), lambda i, p, pt: (pt[i, n_pages - (p + 1) * group + g], 0, 0))
    cache_specs = [cache_spec(g) for g in range(group)]
    return pl.pallas_call(
        functools.partial(_sb_sample_kernel, n_new=t_new, group=group, page=page),
        grid_spec=pltpu.PrefetchScalarGridSpec(
            num_scalar_prefetch=1,
            grid=(n, n_pages // group),
            in_specs=[pl.BlockSpec(ntri.shape, lambda *_: (0, 0)), per_seq(rows_q),
                      pl.BlockSpec(bias_rows.shape, lambda *_: (0, 0)),
                      pl.BlockSpec(head_mask.shape, lambda *_: (0, 0)),
                      per_seq(n_pad), per_seq(n_pad)] + cache_specs + cache_specs,
            out_specs=per_seq(t_new),
            scratch_shapes=[pltpu.VMEM((rows_q, LANES), F32), pltpu.VMEM((rows_q, w), F32),
                            pltpu.VMEM((page, w), BF16)],
        ),
        out_shape=jax.ShapeDtypeStruct((n, t_new, w), F32),
        compiler_params=_params("parallel", "arbitrary"),
        name="sb_sample",
    )(page_table, ntri, q_rows, bias_rows, head_mask, k_new, v_new,
      *([cache_k] * group), *([cache_v] * group))


def _merge_ffn_kernel(x_ref, oa_ref, ob_ref, wga_ref, wgb_ref, wo_ref, ln1g_ref, ln1b_ref,
                      wup_ref, wdn_ref, ln2g_ref, ln2b_ref, y_ref, h_ref, h16_ref, acc_ref):
    j = pl.program_id(1)

    @pl.when(j == 0)
    def _():
        x = x_ref[...]
        x16 = x.astype(BF16)
        mix_in = (_sigmoid(_dot(x16, wga_ref[...])) * oa_ref[...]
                  + _sigmoid(_dot(x16, wgb_ref[...])) * ob_ref[...])
        mix = _dot(mix_in.astype(BF16), wo_ref[...])
        h = _layer_norm(DN_ALPHA * x + mix, ln1g_ref[...], ln1b_ref[...])
        h_ref[...] = h
        h16_ref[...] = h.astype(BF16)
        acc_ref[...] = jnp.zeros_like(acc_ref)

    u = jnp.maximum(_dot(h16_ref[...], wup_ref[...]), 0.0)
    acc_ref[...] += _dot((u * u).astype(BF16), wdn_ref[...])

    @pl.when(j == pl.num_programs(1) - 1)
    def _():
        y_ref[...] = _layer_norm(DN_ALPHA * h_ref[...] + acc_ref[...], ln2g_ref[...], ln2b_ref[...])


def _merge_ffn(x, oa, ob, wga, wgb, wo, ln1g, ln1b, wup, wdn, ln2g, ln2b, tm, tf):
    n, d = x.shape
    dff = wup.shape[1]
    xrow = pl.BlockSpec((tm, d), lambda i, j: (i, 0))
    const = lambda a: pl.BlockSpec(a.shape, lambda i, j: (0,) * a.ndim)
    return pl.pallas_call(
        _merge_ffn_kernel,
        grid=(n // tm, dff // tf),
        in_specs=[xrow, xrow, xrow, const(wga), const(wgb), const(wo), const(ln1g), const(ln1b),
                  pl.BlockSpec((d, tf), lambda i, j: (0, j)), pl.BlockSpec((tf, d), lambda i, j: (j, 0)),
                  const(ln2g), const(ln2b)],
        out_specs=xrow,
        out_shape=jax.ShapeDtypeStruct((n, d), F32),
        scratch_shapes=[pltpu.VMEM((tm, d), F32), pltpu.VMEM((tm, d), BF16), pltpu.VMEM((tm, d), F32)],
        compiler_params=_params("parallel", "arbitrary"),
        name="merge_ffn",
    )(x, oa, ob, wga, wgb, wo, ln1g, ln1b, wup, wdn, ln2g, ln2b)


def _row_tile(n, limit=512):
    t = limit
    while n % t:
        t //= 2
    return t


def kernel(x_prompt, x_sample, cache_sb_k, cache_sb_v, state_gla, page_table, meta_tokens, w_in,
           w_gate_a2, b_gate_a, gla_norm_g, sb_logit_bias, w_o, ln1_g, ln1_b, w_up, w_down, ln2_g, ln2_b):
    bsz, seq, d = x_prompt.shape
    n_dec, t_dec, _ = x_sample.shape
    _, _, n_heads_gla, dk, dv = state_gla.shape
    n_phys, page = cache_sb_k.shape[1], cache_sb_k.shape[2]
    gqk, gv = n_heads_gla * dk, n_heads_gla * dv
    sbw = cache_sb_k.shape[3] * cache_sb_k.shape[4]
    assert n_heads_gla == GLA_HEADS and cache_sb_k.shape[3] == SB_HEADS and w_in.shape[0] == DEPTH == 1
    assert page == PAGE_SIZE and w_gate_a2.shape[1] == GLA_RANK

    w = w_in[0]
    offs = np.cumsum([0, gqk, gqk, gv, gv, GLA_RANK, sbw, sbw, sbw, d, d])
    cols = lambda i: w[:, offs[i]:offs[i + 1]].astype(BF16)
    wq_a, wk_a, wv_a, wr_a, wa_lr, wq_b, wk_b, wv_b, wg_a, wg_b = (cols(i) for i in range(10))
    wa_lr = jnp.pad(wa_lr, ((0, 0), (0, LANES - GLA_RANK)))
    wg2 = jnp.pad(w_gate_a2[0].astype(BF16), ((0, LANES - GLA_RANK), (0, 0)))
    bg = b_gate_a[0][None].astype(F32)
    gain = gla_norm_g[0][None].astype(F32)
    bias = sb_logit_bias[0].astype(F32) * LOG2E
    wo16, wup16, wdn16 = w_o[0].astype(BF16), w_up[0].astype(BF16), w_down[0].astype(BF16)
    row = lambda a: a[0][None].astype(F32)
    ffn_w = (wg_a, wg_b, wo16, row(ln1_g), row(ln1_b), wup16, wdn16, row(ln2_g), row(ln2_b))
    tf = min(1024, wup16.shape[1])

    length = N_META + seq
    lp = -(-length // ROW_ALIGN) * ROW_ALIGN
    front = lp - length
    assert lp % GLA_CHUNK == 0 and lp % SB_BLOCK == 0
    meta = jnp.broadcast_to(meta_tokens.astype(BF16)[None], (bsz, N_META, d))
    xp = jnp.concatenate([jnp.zeros((bsz, front, d), BF16), meta, x_prompt.astype(BF16)], axis=1)
    xp = xp.reshape(bsz * lp, d)
    tm = _row_tile(bsz * lp)
    q_a, k_a, v_a, r_a, la = _proj_gla(xp, wq_a, wk_a, wv_a, wr_a, wa_lr, wg2, bg, tm)
    q_b, k_b, v_b, k_b16, v_b16 = _proj_sb(xp, wq_b, wk_b, wv_b, tm)
    seq3 = lambda a: a.reshape(bsz, lp, a.shape[-1])
    zero_state = jnp.zeros((bsz, GLA_HEADS, dk, dv), F32)
    assert front + N_META == SB_BLOCK and SB_BLOCK % GLA_CHUNK == 0
    o_a, s_p = _gla(seq3(q_a), seq3(k_a), seq3(v_a), seq3(r_a), seq3(la), gain, zero_state, GLA_CHUNK,
                    skip=SB_BLOCK // GLA_CHUNK)
    o_b = _sb_prompt(seq3(q_b), seq3(k_b16), seq3(v_b16), bias, SB_BLOCK)
    y_prompt = _merge_ffn(x_prompt.reshape(bsz * seq, d), o_a.reshape(bsz * seq, gv), o_b.reshape(bsz * seq, sbw),
                          *ffn_w, _row_tile(bsz * seq), tf).reshape(bsz, seq, d)
    new_k_prompt = seq3(k_b)[:, front:].reshape(1, bsz, length, SB_HEADS, sbw // SB_HEADS)
    new_v_prompt = seq3(v_b)[:, front:].reshape(1, bsz, length, SB_HEADS, sbw // SB_HEADS)
    new_gla_prompt = s_p[None].astype(state_gla.dtype)

    ns = n_dec * t_dec
    xs = x_sample.reshape(ns, d)
    tms = _row_tile(ns)
    q_a, k_a, v_a, r_a, la = _proj_gla(xs, wq_a, wk_a, wv_a, wr_a, wa_lr, wg2, bg, tms)
    q_b, k_b, v_b, _, _ = _proj_sb(xs, wq_b, wk_b, wv_b, tms)
    pad3 = lambda a: jnp.pad(a.reshape(n_dec, t_dec, a.shape[-1]), ((0, 0), (0, SAMPLE_PAD - t_dec), (0, 0)))
    o_a, s_s = _gla(pad3(q_a), pad3(k_a), pad3(v_a), pad3(r_a), pad3(la), gain,
                    state_gla[0].astype(F32), SAMPLE_PAD)
    o_a = o_a[:, :t_dec].reshape(ns, gv)
    dec3 = lambda a: a.reshape(n_dec, t_dec, a.shape[-1])
    cache_t = lambda c: jnp.transpose(c[0], (0, 2, 3, 1)).reshape(n_phys, sbw, page)
    o_b = _sb_sample(dec3(q_b).astype(F32), dec3(k_b), dec3(v_b), cache_t(cache_sb_k), cache_t(cache_sb_v),
                     page_table, bias).reshape(ns, sbw)
    y_sample = _merge_ffn(xs, o_a, o_b, *ffn_w, tms, tf).reshape(n_dec, t_dec, d)
    new_k_sample = k_b.reshape(1, n_dec, t_dec, SB_HEADS, sbw // SB_HEADS)
    new_v_sample = v_b.reshape(1, n_dec, t_dec, SB_HEADS, sbw // SB_HEADS)
    new_gla_sample = s_s[None].astype(state_gla.dtype)

    return (y_prompt, y_sample, new_k_prompt, new_v_prompt, new_gla_prompt,
            new_k_sample, new_v_sample, new_gla_sample)
```

```python
import functools

import numpy as np
import jax
import jax.numpy as jnp
from jax import lax
from jax.experimental import pallas as pl
from jax.experimental.pallas import tpu as pltpu

F32 = jnp.float32
BF16 = jnp.bfloat16

N_META = 16
GLA_HEADS = 4
GLA_RANK = 16
GLA_TAU = 16.0
SB_HEADS = 16
PAGE_SIZE = 128
LN_EPS = 1e-5
DEPTH = 1
DN_ALPHA = (2.0 * DEPTH) ** 0.25
LOG2E = 1.4426950408889634

LANES = 128
MXU_DIM = 256
VMEM_LIMIT_BYTES = 56 * 1024 * 1024

GLA_CHUNK = 128
SB_BLOCK = 256
ROW_ALIGN = 256
SAMPLE_PAD = 16
GLA_SAMPLE_SEQS = 4


def _dot(a, b):
    return jnp.dot(a, b, preferred_element_type=F32)


def _dot_nt(a, b):
    return lax.dot_general(a, b, (((1,), (1,)), ((), ())), preferred_element_type=F32)


def _softplus(z):
    return jnp.maximum(z, 0.0) + jnp.log(1.0 + jnp.exp(-jnp.abs(z)))


def _sigmoid(z):
    return 1.0 / (1.0 + jnp.exp(-z))


def _layer_norm(x, g, b):
    mu = jnp.mean(x, axis=-1, keepdims=True)
    xc = x - mu
    var = jnp.mean(xc * xc, axis=-1, keepdims=True)
    return xc * lax.rsqrt(var + LN_EPS) * g + b


def _params(*sem):
    return pltpu.CompilerParams(dimension_semantics=sem, vmem_limit_bytes=VMEM_LIMIT_BYTES)


def _full(shape):
    return pl.BlockSpec(shape, lambda *_: (0,) * len(shape))


def _proj_gla_kernel(x_ref, wq_ref, wk_ref, wv_ref, wr_ref, wa_ref, wg_ref, bg_ref,
                     q_ref, k_ref, v_ref, r_ref, la_ref, *, q_scale):
    x = x_ref[...].astype(BF16)
    q_ref[...] = _dot(x, wq_ref[...]) * q_scale
    k_ref[...] = _dot(x, wk_ref[...])
    v_ref[...] = _dot(x, wv_ref[...]).astype(BF16)
    r = _dot(x, wr_ref[...])
    r_ref[...] = r * _sigmoid(r)
    a = _dot(x, wa_ref[...]).astype(BF16)
    zg = _dot(a, wg_ref[...]) + bg_ref[...]
    la_ref[...] = -_softplus(-zg) * (1.0 / GLA_TAU)


def _proj_gla(x, wq, wk, wv, wr, wa, wg, bg, tm):
    n, d = x.shape
    gqk, gv = wq.shape[1], wv.shape[1]
    dk = gqk // GLA_HEADS
    row = lambda c: pl.BlockSpec((tm, c), lambda i: (i, 0))
    return pl.pallas_call(
        functools.partial(_proj_gla_kernel, q_scale=dk ** -0.5),
        grid=(n // tm,),
        in_specs=[row(d), _full(wq.shape), _full(wk.shape), _full(wv.shape), _full(wr.shape),
                  _full(wa.shape), _full(wg.shape), _full(bg.shape)],
        out_specs=[row(gqk), row(gqk), row(gv), row(gv), row(gqk)],
        out_shape=[jax.ShapeDtypeStruct((n, gqk), F32), jax.ShapeDtypeStruct((n, gqk), F32),
                   jax.ShapeDtypeStruct((n, gv), BF16), jax.ShapeDtypeStruct((n, gv), F32),
                   jax.ShapeDtypeStruct((n, gqk), F32)],
        compiler_params=_params("parallel"),
        name="proj_gla",
    )(x, wq, wk, wv, wr, wa, wg, bg)


def _proj_sb_kernel(x_ref, wq_ref, wk_ref, wv_ref, q_ref, k_ref, v_ref, k16_ref, v16_ref, *, q_scale):
    x = x_ref[...].astype(BF16)
    q_ref[...] = (_dot(x, wq_ref[...]) * q_scale).astype(BF16)
    k = _dot(x, wk_ref[...])
    k_ref[...] = k
    k16_ref[...] = k.astype(BF16)
    v = _dot(x, wv_ref[...])
    v_ref[...] = v
    v16_ref[...] = v.astype(BF16)


def _proj_sb(x, wq, wk, wv, tm):
    n, d = x.shape
    w = wq.shape[1]
    dh = w // SB_HEADS
    row = pl.BlockSpec((tm, w), lambda i: (i, 0))
    return pl.pallas_call(
        functools.partial(_proj_sb_kernel, q_scale=dh ** -0.5 * LOG2E),
        grid=(n // tm,),
        in_specs=[pl.BlockSpec((tm, d), lambda i: (i, 0)), _full(wq.shape), _full(wk.shape), _full(wv.shape)],
        out_specs=[row, row, row, row, row],
        out_shape=[jax.ShapeDtypeStruct((n, w), BF16), jax.ShapeDtypeStruct((n, w), F32),
                   jax.ShapeDtypeStruct((n, w), F32), jax.ShapeDtypeStruct((n, w), BF16),
                   jax.ShapeDtypeStruct((n, w), BF16)],
        compiler_params=_params("parallel"),
        name="proj_sb",
    )(x, wq, wk, wv)


def _gla_sum_matrices(c):
    t = np.arange(c)
    j = t[None, :]
    mats = [j <= t[:, None], j > t[:, None]]
    l = 1
    while l < c:
        start = (t // l) * l
        mats.append((j > start[:, None]) & (j <= t[:, None]))
        nxt = np.minimum((t // l + 1) * l, c - 1)
        mats.append((j > t[:, None]) & (j <= nxt[:, None]))
        l *= 2
    return np.concatenate(mats, axis=0).astype(np.float32)


def _gla_kernel(m_ref, q_ref, k_ref, v_ref, r_ref, la_ref, g_ref, s0_ref, o_ref, s_out_ref,
                e_ref, s_ref, *, chunk, n_levels, dk, dv, n_seq):
    c = pl.program_id(1)

    @pl.when(c == 0)
    def _():
        s_ref[...] = s0_ref[...]

    m = m_ref[...]
    for i in range(n_seq):
        e_ref[i] = _dot(m, la_ref[i].astype(BF16))

    rows = lax.broadcasted_iota(jnp.int32, (chunk, chunk), 0)
    cols = lax.broadcasted_iota(jnp.int32, (chunk, chunk), 1)
    level = 31 - lax.clz(rows ^ cols)
    level = jnp.where(cols < rows, level, -1)
    diag = rows == cols

    for i in range(n_seq):
        for h in range(GLA_HEADS):
            ks = slice(h * dk, (h + 1) * dk)
            vs = slice(h * dv, (h + 1) * dv)
            q = q_ref[i][:, ks]
            k = k_ref[i][:, ks]
            v = v_ref[i][:, vs]
            b = e_ref[i, 0:chunk, ks]
            tail = e_ref[i, chunk:2 * chunk, ks]
            scores = jnp.where(diag, _dot_nt(q.astype(BF16), k.astype(BF16)), 0.0)
            for li in range(n_levels):
                eq = e_ref[i, (2 + 2 * li) * chunk:(3 + 2 * li) * chunk, ks]
                ek = e_ref[i, (3 + 2 * li) * chunk:(4 + 2 * li) * chunk, ks]
                r = _dot_nt((q * jnp.exp(eq)).astype(BF16), (k * jnp.exp(ek)).astype(BF16))
                scores = jnp.where(level == li, r, scores)
            s_prev = s_ref[i, h]
            o = _dot(scores.astype(BF16), v) + _dot((q * jnp.exp(b)).astype(BF16), s_prev.astype(BF16))
            k_out_t = (k * jnp.exp(tail)).T.astype(BF16)
            decay_col = jnp.exp(b.T[:, chunk - 1:chunk])
            s_ref[i, h] = decay_col * s_prev + _dot(k_out_t, v)
            o = o * lax.rsqrt(jnp.mean(o * o, axis=-1, keepdims=True) + LN_EPS)
            o_ref[i, :, vs] = o * g_ref[:, vs] * r_ref[i][:, vs]

    @pl.when(c == pl.num_programs(1) - 1)
    def _():
        s_out_ref[...] = s_ref[...]


def _gla(q, k, v, r, la, gain, s0, chunk, n_seq, skip=0):
    nb, length, gqk = q.shape
    assert nb % n_seq == 0
    gv = v.shape[2]
    dk, dv = gqk // GLA_HEADS, gv // GLA_HEADS
    mats = jnp.asarray(_gla_sum_matrices(chunk), dtype=BF16)
    n_levels = (mats.shape[0] // chunk - 2) // 2
    blk = lambda w: pl.BlockSpec((n_seq, chunk, w), lambda b, c: (b, c, 0))
    st = pl.BlockSpec((n_seq, GLA_HEADS, dk, dv), lambda b, c: (b, 0, 0, 0))
    return pl.pallas_call(
        functools.partial(_gla_kernel, chunk=chunk, n_levels=n_levels, dk=dk, dv=dv, n_seq=n_seq),
        grid=(nb // n_seq, length // chunk),
        in_specs=[_full(mats.shape), blk(gqk), blk(gqk), blk(gv), blk(gv), blk(gqk), _full(gain.shape), st],
        out_specs=[pl.BlockSpec((n_seq, chunk, gv), lambda b, c: (b, jnp.maximum(c - skip, 0), 0)), st],
        out_shape=[jax.ShapeDtypeStruct((nb, length - skip * chunk, gv), F32),
                   jax.ShapeDtypeStruct((nb, GLA_HEADS, dk, dv), F32)],
        scratch_shapes=[pltpu.VMEM((n_seq, mats.shape[0], gqk), F32),
                        pltpu.VMEM((n_seq, GLA_HEADS, dk, dv), F32)],
        compiler_params=_params("parallel", "arbitrary"),
        name="gla",
    )(mats, q, k, v, r, la, gain, s0)


def _neg_log2_keep(y):
    return jnp.maximum(y, 0.0) + jnp.log2(1.0 + jnp.exp2(-jnp.abs(y)))


def _neg_suffix_sum_matrix(n):
    j = np.arange(n)
    return jnp.asarray(-(j[:, None] >= j[None, :]).astype(np.float32), dtype=BF16)


def _sb_prompt_kernel(bias_ref, ntri_ref, q_ref, k_ref, v_ref, o_ref, acc_ref, kb_ref, *, blk, dh, unroll):
    hp = pl.program_id(1)
    qi = pl.program_id(2) + 1
    lane = lax.broadcasted_iota(jnp.int32, (blk, LANES), 1)

    @pl.when(pl.program_id(2) == 0)
    def _():
        def pair(h):
            b = jnp.full((16, LANES), bias_ref[2 * hp + h], F32)
            hi = b.astype(BF16).astype(F32)
            return hi, (b - hi).astype(BF16).astype(F32)
        (h0, l0), (h1, l1) = pair(0), pair(1)
        ln = lax.broadcasted_iota(jnp.int32, (16, LANES), 1)
        feat = jnp.where(ln == 0, h0, jnp.where(ln == 1, l0, jnp.where(ln == 2, h1, jnp.where(ln == 3, l1, 0.0))))
        kb_ref[:, 0:LANES] = k_ref[0]
        kb_ref[:, LANES:2 * LANES] = jnp.broadcast_to(feat[0:1].astype(BF16), (kb_ref.shape[0], LANES))

    q2 = q_ref[0]
    zero = jnp.zeros_like(q2)
    ones0 = jnp.where(lane < 2, 1.0, 0.0).astype(BF16)
    ones1 = jnp.where(lane < 2, 0.0, jnp.where(lane < 4, 1.0, 0.0)).astype(BF16)
    q = jnp.concatenate(
        [jnp.concatenate([jnp.where(lane < dh, q2, zero), ones0], axis=1),
         jnp.concatenate([jnp.where(lane >= dh, q2, zero), ones1], axis=1)], axis=0)
    rows = lax.broadcasted_iota(jnp.int32, (2 * blk, blk), 0)
    cols = lax.broadcasted_iota(jnp.int32, (2 * blk, blk), 1)
    strictly_before = cols < (rows & (blk - 1))

    def logits(j):
        start = pl.multiple_of(j * blk, blk)
        return _dot_nt(q, kb_ref[pl.ds(start, blk), :])

    def suffix(y, valid):
        sp = _neg_log2_keep(y)
        if valid is not None:
            sp = jnp.where(valid, sp, 0.0)
        return _dot(sp.astype(BF16), ntri_ref[...])

    def weighted_values(j, y, cs, carry, valid):
        a = jnp.exp2(y + cs + carry)
        if valid is not None:
            a = jnp.where(valid, a, 0.0)
        start = pl.multiple_of(j * blk, blk)
        return _dot(a.astype(BF16), v_ref[0, pl.ds(start, blk), :])

    y, y1 = logits(qi), logits(qi - 1)
    cs, cs1 = suffix(y, strictly_before), suffix(y1, None)
    carry = cs[:, 0:1]
    acc_ref[...] = (weighted_values(qi, y, cs, 0.0, strictly_before)
                    + weighted_values(qi - 1, y1, cs1, carry, None))
    carry = carry + cs1[:, 0:1]

    def make_body(n, first):
        def body(it, carry):
            js = [first - it * n - u for u in range(n)]
            ys = [logits(j) for j in js]
            css = [suffix(y, None) for y in ys]
            for j, y, cs in zip(js, ys, css):
                acc_ref[...] += weighted_values(j, y, cs, carry, None)
                carry = carry + cs[:, 0:1]
            return carry
        return body

    remaining, n = qi - 1, unroll
    while n >= 1:
        trips = remaining // n
        carry = lax.fori_loop(0, trips, make_body(n, remaining - 1), carry)
        remaining = remaining - trips * n
        n //= 2
    acc = acc_ref[...]
    o_ref[0] = jnp.where(lane < dh, acc[:blk], acc[blk:])


def _sb_prompt(q, k, v, bias, blk, unroll=8):
    bsz, length, w = q.shape
    dh = w // SB_HEADS
    assert LANES == 2 * dh and blk & (blk - 1) == 0
    ntri = _neg_suffix_sum_matrix(blk)
    kvspec = pl.BlockSpec((1, length, LANES), lambda b, h, i, *_: (b, 0, h))
    return pl.pallas_call(
        functools.partial(_sb_prompt_kernel, blk=blk, dh=dh, unroll=unroll),
        grid_spec=pltpu.PrefetchScalarGridSpec(
            num_scalar_prefetch=1,
            grid=(bsz, w // LANES, length // blk - 1),
            in_specs=[pl.BlockSpec(ntri.shape, lambda *_: (0, 0)),
                      pl.BlockSpec((1, blk, LANES), lambda b, h, i, *_: (b, i + 1, h)), kvspec, kvspec],
            out_specs=pl.BlockSpec((1, blk, LANES), lambda b, h, i, *_: (b, i, h)),
            scratch_shapes=[pltpu.VMEM((2 * blk, LANES), F32), pltpu.VMEM((length, 2 * LANES), BF16)],
        ),
        out_shape=jax.ShapeDtypeStruct((bsz, length - blk, w), F32),
        compiler_params=_params("parallel", "parallel", "arbitrary"),
        name="sb_prompt",
    )(bias, ntri, q, k, v)


def _sb_sample_kernel(pt_ref, ntri_ref, q_ref, bias_ref, hm_ref, kn_ref, vn_ref, *rest, n_new, group, page):
    del pt_ref
    kc_refs, vc_refs = rest[:group], rest[group:2 * group]
    o_ref, carry_ref, acc_ref, pad_ref = rest[2 * group:]
    p = pl.program_id(1)
    rows_q = q_ref.shape[1]
    q = q_ref[0]
    bias = bias_ref[...]
    blk = ntri_ref.shape[0]
    per_blk = blk // page
    n_blk = group // per_blk

    @pl.when(p == 0)
    def _():
        n_pad = kn_ref.shape[1]
        pad_ref[...] = jnp.zeros_like(pad_ref)
        pad_ref[0:n_pad, :] = kn_ref[0].astype(BF16)
        y = _dot_nt(q, pad_ref[...]) + bias
        s_idx = lax.broadcasted_iota(jnp.int32, y.shape, 1)
        t_idx = lax.broadcasted_iota(jnp.int32, y.shape, 0) // SB_HEADS
        valid = s_idx < t_idx
        sp = jnp.where(valid, _neg_log2_keep(y), 0.0)
        cs = _dot(sp.astype(BF16), ntri_ref[0:page, 0:page])
        a = jnp.where(valid, jnp.exp2(y + cs), 0.0)
        pad_ref[0:n_pad, :] = vn_ref[0].astype(BF16)
        carry_ref[...] = jnp.broadcast_to(cs[:, 0:1], carry_ref.shape)
        acc_ref[...] = _dot(a.astype(BF16), pad_ref[...])

    k_cat = jnp.concatenate([r[0].astype(BF16) for r in kc_refs], axis=1)
    y = _dot(q, k_cat) + jnp.concatenate([bias] * group, axis=1)
    sp = _neg_log2_keep(y).astype(BF16)
    sp_rows = jnp.concatenate([sp[:, b * blk:(b + 1) * blk] for b in range(n_blk)], axis=0)
    cs_rows = _dot(sp_rows, ntri_ref[...])
    carry = carry_ref[:, 0:1]
    a_blocks = [None] * n_blk
    for b in reversed(range(n_blk)):
        cs = cs_rows[b * rows_q:(b + 1) * rows_q]
        a_blocks[b] = jnp.exp2(y[:, b * blk:(b + 1) * blk] + cs + carry).astype(BF16)
        carry = carry + cs[:, 0:1]
    carry_ref[...] = jnp.broadcast_to(carry, carry_ref.shape)
    v_cat = jnp.concatenate([r[0].astype(BF16) for r in vc_refs], axis=1)
    acc_ref[...] += _dot_nt(jnp.concatenate(a_blocks, axis=1), v_cat)

    @pl.when(p == pl.num_programs(1) - 1)
    def _():
        acc = acc_ref[...].reshape(n_new, SB_HEADS, acc_ref.shape[1])
        o_ref[0] = jnp.sum(acc * hm_ref[...][None], axis=1)


def _sb_sample(q, k_new, v_new, cache_k, cache_v, page_table, bias, group=16):
    n, t_new, w = q.shape
    dh = w // SB_HEADS
    n_pages = page_table.shape[1]
    page = cache_k.shape[2]
    blk = MXU_DIM
    assert n_pages % group == 0 and (group * page) % blk == 0 and blk % page == 0
    head_mask = (jnp.arange(w)[None, :] // dh == jnp.arange(SB_HEADS)[:, None]).astype(F32)
    q_rows = (q[:, :, None, :] * head_mask[None, None]).reshape(n, t_new * SB_HEADS, w).astype(BF16)
    bias_rows = jnp.broadcast_to(jnp.tile(bias, t_new)[:, None], (t_new * SB_HEADS, page)).astype(F32)
    n_pad = 8
    k_new = jnp.pad(k_new, ((0, 0), (0, n_pad - t_new), (0, 0)))
    v_new = jnp.pad(v_new, ((0, 0), (0, n_pad - t_new), (0, 0)))
    ntri = _neg_suffix_sum_matrix(blk)
    rows_q = t_new * SB_HEADS
    per_seq = lambda r: pl.BlockSpec((1, r, w), lambda i, p, pt: (i, 0, 0))
    cache_spec = lambda g: pl.BlockSpec(
        (1, w, page), lambda i, p, pt: (pt[i, n_pages - (p + 1) * group + g], 0, 0))
    cache_specs = [cache_spec(g) for g in range(group)]
    return pl.pallas_call(
        functools.partial(_sb_sample_kernel, n_new=t_new, group=group, page=page),
        grid_spec=pltpu.PrefetchScalarGridSpec(
            num_scalar_prefetch=1,
            grid=(n, n_pages // group),
            in_specs=[pl.BlockSpec(ntri.shape, lambda *_: (0, 0)), per_seq(rows_q),
                      pl.BlockSpec(bias_rows.shape, lambda *_: (0, 0)),
                      pl.BlockSpec(head_mask.shape, lambda *_: (0, 0)),
                      per_seq(n_pad), per_seq(n_pad)] + cache_specs + cache_specs,
            out_specs=per_seq(t_new),
            scratch_shapes=[pltpu.VMEM((rows_q, LANES), F32), pltpu.VMEM((rows_q, w), F32),
                            pltpu.VMEM((page, w), BF16)],
        ),
        out_shape=jax.ShapeDtypeStruct((n, t_new, w), F32),
        compiler_params=_params("parallel", "arbitrary"),
        name="sb_sample",
    )(page_table, ntri, q_rows, bias_rows, head_mask, k_new, v_new,
      *([cache_k] * group), *([cache_v] * group))


def _merge_ffn_kernel(x_ref, oa_ref, ob_ref, wga_ref, wgb_ref, wo_ref, ln1g_ref, ln1b_ref,
                      wup_ref, wdn_ref, ln2g_ref, ln2b_ref, y_ref, h_ref, h16_ref, acc_ref):
    j = pl.program_id(1)

    @pl.when(j == 0)
    def _():
        x = x_ref[...]
        x16 = x.astype(BF16)
        mix_in = (_sigmoid(_dot(x16, wga_ref[...])) * oa_ref[...]
                  + _sigmoid(_dot(x16, wgb_ref[...])) * ob_ref[...])
        mix = _dot(mix_in.astype(BF16), wo_ref[...])
        h = _layer_norm(DN_ALPHA * x + mix, ln1g_ref[...], ln1b_ref[...])
        h_ref[...] = h
        h16_ref[...] = h.astype(BF16)
        acc_ref[...] = jnp.zeros_like(acc_ref)

    u = jnp.maximum(_dot(h16_ref[...], wup_ref[...]), 0.0)
    acc_ref[...] += _dot((u * u).astype(BF16), wdn_ref[...])

    @pl.when(j == pl.num_programs(1) - 1)
    def _():
        y_ref[...] = _layer_norm(DN_ALPHA * h_ref[...] + acc_ref[...], ln2g_ref[...], ln2b_ref[...])


def _merge_ffn(x, oa, ob, wga, wgb, wo, ln1g, ln1b, wup, wdn, ln2g, ln2b, tm, tf):
    n, d = x.shape
    dff = wup.shape[1]
    xrow = pl.BlockSpec((tm, d), lambda i, j: (i, 0))
    const = lambda a: pl.BlockSpec(a.shape, lambda i, j: (0,) * a.ndim)
    return pl.pallas_call(
        _merge_ffn_kernel,
        grid=(n // tm, dff // tf),
        in_specs=[xrow, xrow, xrow, const(wga), const(wgb), const(wo), const(ln1g), const(ln1b),
                  pl.BlockSpec((d, tf), lambda i, j: (0, j)), pl.BlockSpec((tf, d), lambda i, j: (j, 0)),
                  const(ln2g), const(ln2b)],
        out_specs=xrow,
        out_shape=jax.ShapeDtypeStruct((n, d), F32),
        scratch_shapes=[pltpu.VMEM((tm, d), F32), pltpu.VMEM((tm, d), BF16), pltpu.VMEM((tm, d), F32)],
        compiler_params=_params("parallel", "arbitrary"),
        name="merge_ffn",
    )(x, oa, ob, wga, wgb, wo, ln1g, ln1b, wup, wdn, ln2g, ln2b)


def _row_tile(n, limit=512):
    t = limit
    while n % t:
        t //= 2
    return t


def kernel(x_prompt, x_sample, cache_sb_k, cache_sb_v, state_gla, page_table, meta_tokens, w_in,
           w_gate_a2, b_gate_a, gla_norm_g, sb_logit_bias, w_o, ln1_g, ln1_b, w_up, w_down, ln2_g, ln2_b):
    bsz, seq, d = x_prompt.shape
    n_dec, t_dec, _ = x_sample.shape
    _, _, n_heads_gla, dk, dv = state_gla.shape
    n_phys, page = cache_sb_k.shape[1], cache_sb_k.shape[2]
    gqk, gv = n_heads_gla * dk, n_heads_gla * dv
    sbw = cache_sb_k.shape[3] * cache_sb_k.shape[4]
    assert n_heads_gla == GLA_HEADS and cache_sb_k.shape[3] == SB_HEADS and w_in.shape[0] == DEPTH == 1
    assert page == PAGE_SIZE and w_gate_a2.shape[1] == GLA_RANK

    w = w_in[0]
    offs = np.cumsum([0, gqk, gqk, gv, gv, GLA_RANK, sbw, sbw, sbw, d, d])
    cols = lambda i: w[:, offs[i]:offs[i + 1]].astype(BF16)
    wq_a, wk_a, wv_a, wr_a, wa_lr, wq_b, wk_b, wv_b, wg_a, wg_b = (cols(i) for i in range(10))
    wa_lr = jnp.pad(wa_lr, ((0, 0), (0, LANES - GLA_RANK)))
    wg2 = jnp.pad(w_gate_a2[0].astype(BF16), ((0, LANES - GLA_RANK), (0, 0)))
    bg = b_gate_a[0][None].astype(F32)
    gain = gla_norm_g[0][None].astype(F32)
    bias = sb_logit_bias[0].astype(F32) * LOG2E
    wo16, wup16, wdn16 = w_o[0].astype(BF16), w_up[0].astype(BF16), w_down[0].astype(BF16)
    row = lambda a: a[0][None].astype(F32)
    ffn_w = (wg_a, wg_b, wo16, row(ln1_g), row(ln1_b), wup16, wdn16, row(ln2_g), row(ln2_b))
    tf = min(1024, wup16.shape[1])

    length = N_META + seq
    lp = -(-length // ROW_ALIGN) * ROW_ALIGN
    front = lp - length
    assert lp % GLA_CHUNK == 0 and lp % SB_BLOCK == 0
    meta = jnp.broadcast_to(meta_tokens.astype(BF16)[None], (bsz, N_META, d))
    xp = jnp.concatenate([jnp.zeros((bsz, front, d), BF16), meta, x_prompt.astype(BF16)], axis=1)
    xp = xp.reshape(bsz * lp, d)
    tm = _row_tile(bsz * lp)
    q_a, k_a, v_a, r_a, la = _proj_gla(xp, wq_a, wk_a, wv_a, wr_a, wa_lr, wg2, bg, tm)
    q_b, k_b, v_b, k_b16, v_b16 = _proj_sb(xp, wq_b, wk_b, wv_b, tm)
    seq3 = lambda a: a.reshape(bsz, lp, a.shape[-1])
    zero_state = jnp.zeros((bsz, GLA_HEADS, dk, dv), F32)
    assert front + N_META == SB_BLOCK and SB_BLOCK % GLA_CHUNK == 0
    o_a, s_p = _gla(seq3(q_a), seq3(k_a), seq3(v_a), seq3(r_a), seq3(la), gain, zero_state, GLA_CHUNK,
                    n_seq=bsz, skip=SB_BLOCK // GLA_CHUNK)
    o_b = _sb_prompt(seq3(q_b), seq3(k_b16), seq3(v_b16), bias, SB_BLOCK)
    y_prompt = _merge_ffn(x_prompt.reshape(bsz * seq, d), o_a.reshape(bsz * seq, gv), o_b.reshape(bsz * seq, sbw),
                          *ffn_w, _row_tile(bsz * seq), tf).reshape(bsz, seq, d)
    new_k_prompt = seq3(k_b)[:, front:].reshape(1, bsz, length, SB_HEADS, sbw // SB_HEADS)
    new_v_prompt = seq3(v_b)[:, front:].reshape(1, bsz, length, SB_HEADS, sbw // SB_HEADS)
    new_gla_prompt = s_p[None].astype(state_gla.dtype)

    ns = n_dec * t_dec
    xs = x_sample.reshape(ns, d)
    tms = _row_tile(ns)
    q_a, k_a, v_a, r_a, la = _proj_gla(xs, wq_a, wk_a, wv_a, wr_a, wa_lr, wg2, bg, tms)
    q_b, k_b, v_b, _, _ = _proj_sb(xs, wq_b, wk_b, wv_b, tms)
    pad3 = lambda a: jnp.pad(a.reshape(n_dec, t_dec, a.shape[-1]), ((0, 0), (0, SAMPLE_PAD - t_dec), (0, 0)))
    o_a, s_s = _gla(pad3(q_a), pad3(k_a), pad3(v_a), pad3(r_a), pad3(la), gain,
                    state_gla[0].astype(F32), SAMPLE_PAD, n_seq=GLA_SAMPLE_SEQS)
    o_a = o_a[:, :t_dec].reshape(ns, gv)
    dec3 = lambda a: a.reshape(n_dec, t_dec, a.shape[-1])
    cache_t = lambda c: jnp.transpose(c[0], (0, 2, 3, 1)).reshape(n_phys, sbw, page)
    o_b = _sb_sample(dec3(q_b).astype(F32), dec3(k_b), dec3(v_b), cache_t(cache_sb_k), cache_t(cache_sb_v),
                     page_table, bias).reshape(ns, sbw)
    y_sample = _merge_ffn(xs, o_a, o_b, *ffn_w, tms, tf).reshape(n_dec, t_dec, d)
    new_k_sample = k_b.reshape(1, n_dec, t_dec, SB_HEADS, sbw // SB_HEADS)
    new_v_sample = v_b.reshape(1, n_dec, t_dec, SB_HEADS, sbw // SB_HEADS)
    new_gla_sample = s_s[None].astype(state_gla.dtype)

    return (y_prompt, y_sample, new_k_prompt, new_v_prompt, new_gla_prompt,
            new_k_sample, new_v_sample, new_gla_sample)
```
